```python
import math
import jax
import jax.numpy as jnp
from jax import lax
import numpy as np

D_MODEL = 2048
BATCH = 4
SEQ = 2048
DEPTH = 4
DEC_BATCH = 8
DEC_SEQ = 1
PAST_LEN = 16384
PAGE_SIZE = 128

HEAD_DIM = 128
N_HEADS = D_MODEL // HEAD_DIM
NSA_HEADS = N_HEADS // 2
SB_HEADS = N_HEADS // 4
MOBA_HEADS = N_HEADS - NSA_HEADS - SB_HEADS
NSA_GROUP = 4
NSA_KV = NSA_HEADS // NSA_GROUP
MIX_WIDTH = N_HEADS * HEAD_DIM
NSA_KV_COLS = 6 * NSA_KV * HEAD_DIM
SB_KV_COLS = 2 * SB_HEADS * HEAD_DIM
MOBA_KV_COLS = 2 * MOBA_HEADS * HEAD_DIM
GATE_COLS = 3 * NSA_HEADS
IN_COLS = MIX_WIDTH + NSA_KV_COLS + SB_KV_COLS + MOBA_KV_COLS + GATE_COLS
D_FF = 4 * D_MODEL
NSA_BLOCK = 64
NSA_N_SEL = 16
NSA_WINDOW = 512
MOBA_BLOCK = 256
MOBA_TOPK = 3
REL_BUCKETS = 32
REL_MAX_DIST = 2048
Q_BLOCK = 128
GATHER_Q_BLOCK = 32
RMS_EPS = 1e-6
ATTN_SCALE = HEAD_DIM ** -0.5
NEG = -1e30
FORCED = 1e4

kernel_name = 'hymba_nsa_stickbreak_moba_step'


def _rms_norm(x, g):
    xf = x.astype(jnp.float32)
    y = xf * lax.rsqrt(jnp.mean(xf * xf, axis=-1, keepdims=True) + RMS_EPS)
    return (y * g.astype(jnp.float32)).astype(x.dtype)


def _masked_softmax(logits, mask):
    z = jnp.where(mask, logits.astype(jnp.float32), NEG)
    z = z - jnp.max(z, axis=-1, keepdims=True)
    e = jnp.where(mask, jnp.exp(z), 0.0)
    return e / jnp.maximum(jnp.sum(e, axis=-1, keepdims=True), 1e-30)


def _rel_bucket(dist):
    exact = REL_BUCKETS // 2
    d = jnp.maximum(dist, 0)
    df = jnp.maximum(d, 1).astype(jnp.float32)
    far = exact + (jnp.log(df / exact) / math.log(REL_MAX_DIST / exact) * (REL_BUCKETS - exact)).astype(jnp.int32)
    return jnp.where(d < exact, d, jnp.minimum(far, REL_BUCKETS - 1))


def _chunk(n, blk):
    return blk if n % blk == 0 else n


def _to_chunks(a, axis, size):
    n = a.shape[axis] // size
    a = a.reshape(a.shape[:axis] + (n, size) + a.shape[axis + 1:])
    return jnp.moveaxis(a, axis, 0)


def _from_chunks(a, axis):
    a = jnp.moveaxis(a, 0, axis)
    return a.reshape(a.shape[:axis] + (a.shape[axis] * a.shape[axis + 1],) + a.shape[axis + 2:])


def _gather_pages(pool, page_table):
    g = pool[page_table]
    return g.reshape((g.shape[0], g.shape[1] * g.shape[2]) + g.shape[3:])


def _compress(blk, pe, alpha, w1, w2):
    z = jnp.einsum('bnlgd,l->bngd', blk + pe[:, None, :], alpha)
    hid = jax.nn.silu(jnp.einsum('bngd,de->bnge', z, w1))
    return jnp.einsum('bnge,ed->bngd', hid, w2)


def _window(qg, k, v, q_off, tbl):
    B, T, G, J, dh = qg.shape
    C = _chunk(T, Q_BLOCK)
    P = NSA_WINDOW
    span = C + P
    kp = jnp.pad(k, ((0, 0), (P, 0), (0, 0), (0, 0)))
    vp = jnp.pad(v, ((0, 0), (P, 0), (0, 0), (0, 0)))
    dist = (P + jnp.arange(C, dtype=jnp.int32))[:, None] - jnp.arange(span, dtype=jnp.int32)[None, :]
    band = (dist >= 0) & (dist < P)
    bias = tbl[_rel_bucket(dist)].transpose(2, 3, 0, 1)

    def one(c):
        q0 = c * C
        qc = lax.dynamic_slice_in_dim(qg, q0, C, axis=1)
        kc = lax.dynamic_slice_in_dim(kp, q0 + q_off, span, axis=1)
        vc = lax.dynamic_slice_in_dim(vp, q0 + q_off, span, axis=1)
        mask = band & ((q0 + q_off + jnp.arange(span, dtype=jnp.int32)) >= P)[None, :]
        logit = jnp.einsum('bcgjd,bsgd->bgjcs', qc, kc, preferred_element_type=jnp.float32) * ATTN_SCALE + bias
        pr = _masked_softmax(logit, mask)
        return jnp.einsum('bgjcs,bsgd->bcgjd', pr, vc)

    out = lax.map(one, jnp.arange(T // C, dtype=jnp.int32))
    return _from_chunks(out, 1)


def _nsa(qg, rows, win_kv, win_off, gates, q_pos0, cmp_pe, cmp_alpha, cmp_w1, cmp_w2, tbl):
    B, T, G, J, dh = qg.shape
    S = rows.shape[1]
    L = NSA_BLOCK
    nb = -(-S // L)
    rows_p = jnp.pad(rows, ((0, 0), (0, nb * L - S), (0, 0), (0, 0), (0, 0)))
    blk = rows_p.reshape(B, nb, L, 4, G, dh)
    t = q_pos0 + jnp.arange(T, dtype=jnp.int32)
    bidx = jnp.arange(nb, dtype=jnp.int32)

    kc = _compress(blk[:, :, :, 0], cmp_pe[0], cmp_alpha[0], cmp_w1[0], cmp_w2[0])
    vc = _compress(blk[:, :, :, 1], cmp_pe[1], cmp_alpha[1], cmp_w1[1], cmp_w2[1])
    dist_c = t[:, None] - (bidx * L + (L - 1))[None, :]
    complete = dist_c >= 0
    bias_c = tbl[_rel_bucket(dist_c)].transpose(2, 3, 0, 1)
    logit_c = jnp.einsum('btgjd,bngd->bgjtn', qg, kc, preferred_element_type=jnp.float32) * ATTN_SCALE + bias_c
    p_c = _masked_softmax(logit_c, complete)
    o_cmp = jnp.einsum('bgjtn,bngd->btgjd', p_c, vc)

    imp = jnp.sum(p_c, axis=2)
    forced = (bidx[None, :] == (t // L)[:, None]) | (bidx[None, :] == 0)
    score = jnp.where(forced, FORCED, jnp.where(complete, imp, NEG))
    top_s, top_i = lax.top_k(score, min(NSA_N_SEL, nb))
    sel_ok = top_s > 0.5 * NEG
    nsel = top_i.shape[-1]
    slc_kb = blk[:, :, :, 2].transpose(0, 3, 1, 2, 4)
    slc_vb = blk[:, :, :, 3].transpose(0, 3, 1, 2, 4)
    qh = qg.transpose(0, 2, 1, 3, 4)
    tbl_g = tbl.transpose(1, 0, 2)
    bi = jnp.arange(B)[:, None, None, None]
    gi = jnp.arange(G)[None, :, None, None]
    C = _chunk(T, GATHER_Q_BLOCK)

    def sel_chunk(args):
        q_c, idx_c, ok_c, t_c = args
        kg = slc_kb[bi, gi, idx_c]
        vg = slc_vb[bi, gi, idx_c]
        pos = idx_c[..., None] * L + jnp.arange(L, dtype=jnp.int32)
        dist = t_c[None, None, :, None, None] - pos
        mask = ok_c[..., None] & (dist >= 0)
        bias = tbl_g[gi[..., None], _rel_bucket(dist)].transpose(0, 1, 2, 5, 3, 4)
        logit = jnp.einsum('bgcjd,bgcksd->bgcjks', q_c, kg, preferred_element_type=jnp.float32) * ATTN_SCALE + bias
        pr = _masked_softmax(logit.reshape(B, G, C, J, nsel * L), mask.reshape(B, G, C, 1, nsel * L))
        return jnp.einsum('bgcjn,bgcnd->bgcjd', pr, vg.reshape(B, G, C, nsel * L, dh))

    o_slc = lax.map(sel_chunk, (_to_chunks(qh, 2, C), _to_chunks(top_i, 2, C),
                                _to_chunks(sel_ok, 2, C), t.reshape(-1, C)))
    o_slc = _from_chunks(o_slc, 2).transpose(0, 2, 1, 3, 4)

    o_win = _window(qg, win_kv[:, :, 0], win_kv[:, :, 1], win_off, tbl)

    o = (gates[:, :, 0, :, :, None] * o_cmp + gates[:, :, 1, :, :, None] * o_slc
         + gates[:, :, 2, :, :, None] * o_win)
    return o.reshape(B, T, G * J * dh)


def _stick_breaking(q, kv, q_pos0):
    B, T, H, dh = q.shape
    S = kv.shape[1]
    k = kv[:, :, 0]
    v = kv[:, :, 1]
    C = _chunk(T, Q_BLOCK)
    kpos = jnp.arange(S, dtype=jnp.int32)
    t = q_pos0 + jnp.arange(T, dtype=jnp.int32)

    def one(args):
        q_c, t_c = args
        z = jnp.einsum('bchd,bshd->bhcs', q_c, k, preferred_element_type=jnp.float32) * ATTN_SCALE
        mask = kpos[None, :] < t_c[:, None]
        log_keep = jnp.where(mask, jax.nn.log_sigmoid(-z), 0.0)
        later = lax.cumsum(log_keep, axis=3, reverse=True) - log_keep
        a = jnp.where(mask, jnp.exp(jax.nn.log_sigmoid(z) + later), 0.0)
        return jnp.einsum('bhcs,bshd->bchd', a, v)

    out = lax.map(one, (_to_chunks(q, 1, C), t.reshape(-1, C)))
    return _from_chunks(out, 1).reshape(B, T, H * dh)


def _moba(q, kv, q_pos0, tbl):
    B, T, H, dh = q.shape
    S = kv.shape[1]
    Lb = MOBA_BLOCK
    nb = -(-S // Lb)
    kvp = jnp.pad(kv, ((0, 0), (0, nb * Lb - S), (0, 0), (0, 0), (0, 0)))
    kb = kvp[:, :, 0].reshape(B, nb, Lb, H, dh)
    vb = kvp[:, :, 1].reshape(B, nb, Lb, H, dh)
    kmean = jnp.mean(kb.astype(jnp.float32), axis=2)
    t = q_pos0 + jnp.arange(T, dtype=jnp.int32)
    own = t // Lb
    past = jnp.arange(nb, dtype=jnp.int32)[None, :] < own[:, None]
    qh = q.transpose(0, 2, 1, 3)
    gs = jnp.einsum('bhtd,bnhd->bhtn', qh.astype(jnp.float32), kmean)
    top_s, top_i = lax.top_k(jnp.where(past, gs, NEG), min(MOBA_TOPK, nb))
    idx = jnp.concatenate([top_i, jnp.broadcast_to(own[None, None, :, None], (B, H, T, 1)).astype(top_i.dtype)], axis=-1)
    ok = jnp.concatenate([top_s > 0.5 * NEG, jnp.ones((B, H, T, 1), dtype=bool)], axis=-1)
    nsel = idx.shape[-1]
    kbh = kb.transpose(0, 3, 1, 2, 4)
    vbh = vb.transpose(0, 3, 1, 2, 4)
    tbl_h = tbl.T
    bi = jnp.arange(B)[:, None, None, None]
    hi = jnp.arange(H)[None, :, None, None]
    C = _chunk(T, GATHER_Q_BLOCK)

    def one(args):
        q_c, idx_c, ok_c, t_c = args
        kg = kbh[bi, hi, idx_c]
        vg = vbh[bi, hi, idx_c]
        pos = idx_c[..., None] * Lb + jnp.arange(Lb, dtype=jnp.int32)
        dist = t_c[None, None, :, None, None] - pos
        mask = ok_c[..., None] & (dist >= 0)
        bias = tbl_h[hi[..., None], _rel_bucket(dist)]
        logit = jnp.einsum('bhcd,bhcksd->bhcks', q_c, kg, preferred_element_type=jnp.float32) * ATTN_SCALE + bias
        pr = _masked_softmax(logit.reshape(B, H, C, nsel * Lb), mask.reshape(B, H, C, nsel * Lb))
        return jnp.einsum('bhcn,bhcnd->bhcd', pr, vg.reshape(B, H, C, nsel * Lb, dh))

    out = lax.map(one, (_to_chunks(qh, 2, C), _to_chunks(idx, 2, C), _to_chunks(ok, 2, C), t.reshape(-1, C)))
    return _from_chunks(out, 2).transpose(0, 2, 1, 3).reshape(B, T, H * dh)


def _block(x, nsa_past, sb_past, moba_past, win_past, q_pos0, g_mix, w_in, cmp_pe, cmp_alpha,
           cmp_w1, cmp_w2, rel_bias, w_out, g_ffn, w_up, w_down):
    B, T, _ = x.shape
    h = _rms_norm(x, g_mix)
    u = jnp.einsum('btd,de->bte', h, w_in)
    o1 = MIX_WIDTH
    o2 = o1 + NSA_KV_COLS
    o3 = o2 + SB_KV_COLS
    o4 = o3 + MOBA_KV_COLS
    q = u[..., :o1].reshape(B, T, N_HEADS, HEAD_DIM)
    nsa_kv = u[..., o1:o2].reshape(B, T, 6, NSA_KV, HEAD_DIM)
    sb_rows = u[..., o2:o3].reshape(B, T, 2, SB_HEADS, HEAD_DIM)
    moba_rows = u[..., o3:o4].reshape(B, T, 2, MOBA_HEADS, HEAD_DIM)
    gates = jax.nn.sigmoid(u[..., o4:].astype(jnp.float32)).reshape(B, T, 3, NSA_KV, NSA_GROUP)
    nsa_rows = nsa_kv[:, :, :4]
    win_rows = nsa_kv[:, :, 4:]
    if nsa_past is None:
        nsa_full, sb_full, moba_full, win_full = nsa_rows, sb_rows, moba_rows, win_rows
        win_off = 0
        win_keep = min(NSA_WINDOW, T)
    else:
        nsa_full = jnp.concatenate([nsa_past, nsa_rows.astype(nsa_past.dtype)], axis=1)
        sb_full = jnp.concatenate([sb_past, sb_rows.astype(sb_past.dtype)], axis=1)
        moba_full = jnp.concatenate([moba_past, moba_rows.astype(moba_past.dtype)], axis=1)
        win_full = jnp.concatenate([win_past, win_rows.astype(win_past.dtype)], axis=1)
        win_off = win_past.shape[1]
        win_keep = win_past.shape[1]
    q_nsa = q[:, :, :NSA_HEADS].reshape(B, T, NSA_KV, NSA_GROUP, HEAD_DIM)
    q_sb = q[:, :, NSA_HEADS:NSA_HEADS + SB_HEADS]
    q_moba = q[:, :, NSA_HEADS + SB_HEADS:]
    tbl_nsa = rel_bias[:, :NSA_HEADS].reshape(REL_BUCKETS, NSA_KV, NSA_GROUP)
    tbl_moba = rel_bias[:, NSA_HEADS:]
    o_nsa = _nsa(q_nsa, nsa_full, win_full, win_off, gates, q_pos0, cmp_pe, cmp_alpha, cmp_w1, cmp_w2, tbl_nsa)
    o_sb = _stick_breaking(q_sb, sb_full, q_pos0)
    o_moba = _moba(q_moba, moba_full, q_pos0, tbl_moba)
    mix = jnp.concatenate([o_nsa.astype(x.dtype), o_sb.astype(x.dtype), o_moba.astype(x.dtype)], axis=-1)
    x = x + jnp.einsum('bte,ed->btd', mix, w_out)
    h2 = _rms_norm(x, g_ffn)
    a = jax.nn.relu(jnp.einsum('btd,df->btf', h2, w_up))
    x = x + jnp.einsum('btf,fd->btd', a * a, w_down)
    new_win = win_full[:, win_full.shape[1] - win_keep:]
    return x, nsa_rows, sb_rows, moba_rows, new_win


def setup_inputs(seed: int = 0) -> dict:
    key = jax.random.key(seed)
    ks = jax.random.split(key, 20)
    f32 = jnp.float32
    n_pages = PAST_LEN // PAGE_SIZE
    n_used = DEC_BATCH * n_pages
    n_pool = n_used + max(1, n_used // 4)
    w_buf = min(NSA_WINDOW, PAST_LEN)

    def nrm(k, shape, s=1.0):
        return s * jax.random.normal(k, shape, f32)

    page_table = jax.random.permutation(ks[6], n_pool)[:n_used].reshape(DEC_BATCH, n_pages).astype(jnp.int32)
    return {
        'x_prompt': nrm(ks[0], (BATCH, SEQ, D_MODEL)),
        'x_sample': nrm(ks[1], (DEC_BATCH, DEC_SEQ, D_MODEL)),
        'cache_nsa': nrm(ks[2], (DEPTH, n_pool, PAGE_SIZE, 4, NSA_KV, HEAD_DIM)),
        'cache_sb': nrm(ks[3], (DEPTH, n_pool, PAGE_SIZE, 2, SB_HEADS, HEAD_DIM)),
        'cache_moba': nrm(ks[4], (DEPTH, n_pool, PAGE_SIZE, 2, MOBA_HEADS, HEAD_DIM)),
        'state_nsa_win': nrm(ks[5], (DEPTH, DEC_BATCH, w_buf, 2, NSA_KV, HEAD_DIM)),
        'page_table': page_table,
        'norm_mix': 1.0 + nrm(ks[7], (DEPTH, D_MODEL), 0.05),
        'w_in': nrm(ks[8], (DEPTH, D_MODEL, IN_COLS), D_MODEL ** -0.5),
        'nsa_cmp_pe': nrm(ks[9], (DEPTH, 2, NSA_BLOCK, HEAD_DIM), 0.5),
        'nsa_cmp_alpha': (1.0 + nrm(ks[10], (DEPTH, 2, NSA_BLOCK), 0.1)) / NSA_BLOCK,
        'nsa_cmp_w1': nrm(ks[11], (DEPTH, 2, HEAD_DIM, HEAD_DIM), HEAD_DIM ** -0.5),
        'nsa_cmp_w2': nrm(ks[12], (DEPTH, 2, HEAD_DIM, HEAD_DIM), HEAD_DIM ** -0.5),
        'rel_bias': nrm(ks[13], (REL_BUCKETS, NSA_HEADS + MOBA_HEADS), 0.5),
        'w_out': nrm(ks[14], (DEPTH, MIX_WIDTH, D_MODEL), MIX_WIDTH ** -0.5),
        'norm_ffn': 1.0 + nrm(ks[15], (DEPTH, D_MODEL), 0.05),
        'w_up': nrm(ks[16], (DEPTH, D_MODEL, D_FF), D_MODEL ** -0.5),
        'w_down': nrm(ks[17], (DEPTH, D_FF, D_MODEL), D_FF ** -0.5),
        'norm_final': 1.0 + nrm(ks[18], (D_MODEL,), 0.05),
    }


def reference(x_prompt, x_sample, cache_nsa, cache_sb, cache_moba, state_nsa_win, page_table,
              norm_mix, w_in, nsa_cmp_pe, nsa_cmp_alpha, nsa_cmp_w1, nsa_cmp_w2, rel_bias,
              w_out, norm_ffn, w_up, w_down, norm_final):
    past_len = page_table.shape[1] * cache_nsa.shape[2]
    xp = x_prompt
    xs = x_sample
    nsa_p, nsa_s, sb_p, sb_s, moba_p, moba_s, win_p, win_s = [], [], [], [], [], [], [], []
    for l in range(DEPTH):
        lw = (norm_mix[l], w_in[l], nsa_cmp_pe[l], nsa_cmp_alpha[l], nsa_cmp_w1[l], nsa_cmp_w2[l],
              rel_bias, w_out[l], norm_ffn[l], w_up[l], w_down[l])
        xp, r_nsa, r_sb, r_moba, r_win = _block(xp, None, None, None, None, 0, *lw)
        nsa_p.append(r_nsa)
        sb_p.append(r_sb)
        moba_p.append(r_moba)
        win_p.append(r_win)
        xs, r_nsa, r_sb, r_moba, r_win = _block(
            xs, _gather_pages(cache_nsa[l], page_table), _gather_pages(cache_sb[l], page_table),
            _gather_pages(cache_moba[l], page_table), state_nsa_win[l], past_len, *lw)
        nsa_s.append(r_nsa)
        sb_s.append(r_sb)
        moba_s.append(r_moba)
        win_s.append(r_win)
    y_prompt = _rms_norm(xp, norm_final)
    y_sample = _rms_norm(xs, norm_final)
    nsa_rows_p = jnp.stack(nsa_p)
    nsa_rows_s = jnp.stack(nsa_s)
    sb_rows_p = jnp.stack(sb_p)
    sb_rows_s = jnp.stack(sb_s)
    moba_rows_p = jnp.stack(moba_p)
    moba_rows_s = jnp.stack(moba_s)
    win_state_p = jnp.stack(win_p)
    win_state_s = jnp.stack(win_s)
    return (y_prompt, y_sample, nsa_rows_p, nsa_rows_s, sb_rows_p, sb_rows_s,
            moba_rows_p, moba_rows_s, win_state_p, win_state_s)
```

```python
import functools
import math

import jax
import jax.numpy as jnp
from jax import lax
from jax.experimental import pallas as pl
from jax.experimental.pallas import tpu as pltpu

F32 = jnp.float32
BF16 = jnp.bfloat16
I32 = jnp.int32

HEAD_DIM = 128
NSA_HEADS = 8
SB_HEADS = 4
MOBA_HEADS = 4
NSA_GROUP = 4
NSA_KV = 2
N_HEADS = NSA_HEADS + SB_HEADS + MOBA_HEADS
MIX_WIDTH = N_HEADS * HEAD_DIM
NSA_ROW_COLS = 4 * NSA_KV * HEAD_DIM
WIN_ROW_COLS = 2 * NSA_KV * HEAD_DIM
SB_ROW_COLS = 2 * SB_HEADS * HEAD_DIM
MOBA_ROW_COLS = 2 * MOBA_HEADS * HEAD_DIM
KV_COLS = NSA_ROW_COLS + WIN_ROW_COLS + SB_ROW_COLS + MOBA_ROW_COLS
GATE_COLS = 3 * NSA_HEADS
NSA_BLOCK = 64
NSA_N_SEL = 16
NSA_WINDOW = 512
MOBA_BLOCK = 256
MOBA_TOPK = 3
REL_BUCKETS = 32
REL_MAX_DIST = 2048
RMS_EPS = 1e-6
ATTN_SCALE = HEAD_DIM ** -0.5
NEG = -1e30
FORCED = 1e4

V7X_VMEM_LIMIT_BYTES = 56 * 1024 * 1024
LANES = 128
COL_TILE = 512
ROW_TILE = 1024
ATT_TILE = 256
SAMPLE_ROWS = 16


def _cparams(sem):
    return pltpu.CompilerParams(dimension_semantics=sem, vmem_limit_bytes=V7X_VMEM_LIMIT_BYTES)


def _dot(a, b):
    return jnp.dot(a, b, preferred_element_type=F32)


def _dot_nt(a, b):
    return lax.dot_general(a, b, (((1,), (1,)), ((), ())), preferred_element_type=F32)


def _dot_hilo(a, b_bf16):
    hi = a.astype(BF16)
    lo = (a - hi.astype(F32)).astype(BF16)
    return _dot(hi, b_bf16) + _dot(lo, b_bf16)


def _iota(shape, dim):
    return lax.broadcasted_iota(I32, shape, dim)


def _rms(x, g):
    return (x * lax.rsqrt(jnp.mean(x * x, axis=-1, keepdims=True) + RMS_EPS)) * g


def _softplus(z):
    return jnp.maximum(z, 0.0) + jnp.log1p(jnp.exp(-jnp.abs(z)))


N_Q_TILES = MIX_WIDTH // COL_TILE
N_KV_TILES = KV_COLS // COL_TILE
KV_TILE_START = {"nsa": 0, "win": 2, "sb": 3, "moba": 5}
KV_TILE_COUNT = {"nsa": 2, "win": 1, "sb": 2, "moba": 2}


def _in_proj_kernel(x_ref, g_ref, w_ref, wg_ref, q_ref, nsa_ref, win_ref, sb_ref, moba_ref,
                    gate_ref, kvb_ref, h_ref):
    j = pl.program_id(1)

    @pl.when(j == 0)
    def _():
        h = _rms(x_ref[...], g_ref[...]).astype(BF16)
        h_ref[...] = h
        gate_ref[...] = jax.nn.sigmoid(_dot(h, wg_ref[...]))

    acc = _dot(h_ref[...], w_ref[...])

    @pl.when(j < N_Q_TILES)
    def _():
        q_ref[...] = (acc * ATTN_SCALE).astype(BF16)

    @pl.when(j >= N_Q_TILES)
    def _():
        kvb_ref[...] = acc.astype(BF16)

    for name, ref in (("nsa", nsa_ref), ("win", win_ref), ("sb", sb_ref), ("moba", moba_ref)):
        lo = N_Q_TILES + KV_TILE_START[name]
        hi = lo + KV_TILE_COUNT[name]

        @pl.when((j >= lo) & (j < hi))
        def _(ref=ref):
            ref[...] = acc


def _in_proj(x, g, w_main, w_gate, layer, tm):
    m, d = x.shape
    n_tiles = N_Q_TILES + N_KV_TILES

    def clamp_map(lo, count):
        return lambda i, j: (i, jnp.clip(j - lo, 0, count - 1))

    out_shape = [
        jax.ShapeDtypeStruct((m, MIX_WIDTH), BF16),
        jax.ShapeDtypeStruct((m, NSA_ROW_COLS), F32),
        jax.ShapeDtypeStruct((m, WIN_ROW_COLS), F32),
        jax.ShapeDtypeStruct((m, SB_ROW_COLS), F32),
        jax.ShapeDtypeStruct((m, MOBA_ROW_COLS), F32),
        jax.ShapeDtypeStruct((m, 2 * LANES), F32),
        jax.ShapeDtypeStruct((m, KV_COLS), BF16),
    ]
    out_specs = [
        pl.BlockSpec((tm, COL_TILE), clamp_map(0, N_Q_TILES)),
        pl.BlockSpec((tm, COL_TILE), clamp_map(N_Q_TILES + KV_TILE_START["nsa"], KV_TILE_COUNT["nsa"])),
        pl.BlockSpec((tm, COL_TILE), clamp_map(N_Q_TILES + KV_TILE_START["win"], KV_TILE_COUNT["win"])),
        pl.BlockSpec((tm, COL_TILE), clamp_map(N_Q_TILES + KV_TILE_START["sb"], KV_TILE_COUNT["sb"])),
        pl.BlockSpec((tm, COL_TILE), clamp_map(N_Q_TILES + KV_TILE_START["moba"], KV_TILE_COUNT["moba"])),
        pl.BlockSpec((tm, 2 * LANES), lambda i, j: (i, 0)),
        pl.BlockSpec((tm, COL_TILE), clamp_map(N_Q_TILES, N_KV_TILES)),
    ]
    return pl.pallas_call(
        _in_proj_kernel,
        grid=(m // tm, n_tiles),
        in_specs=[
            pl.BlockSpec((tm, d), lambda i, j: (i, 0)),
            pl.BlockSpec((None, 1, d), lambda i, j: (layer, 0, 0)),
            pl.BlockSpec((None, d, COL_TILE), lambda i, j: (layer, 0, j)),
            pl.BlockSpec((None, d, 2 * LANES), lambda i, j: (layer, 0, 0)),
        ],
        out_specs=out_specs,
        out_shape=out_shape,
        scratch_shapes=[pltpu.VMEM((tm, d), BF16)],
        compiler_params=_cparams(("arbitrary", "arbitrary")),
        name="in_proj",
    )(x, g, w_main, w_gate)


def _out_proj_kernel(x_ref, a_ref, b_ref, c_ref, w_ref, o_ref):
    n_a = a_ref.shape[1]
    n_b = b_ref.shape[1]
    acc = _dot(a_ref[...], w_ref[0:n_a, :])
    acc += _dot(b_ref[...], w_ref[n_a:n_a + n_b, :])
    acc += _dot(c_ref[...], w_ref[n_a + n_b:, :])
    o_ref[...] = x_ref[...] + acc


def _out_proj(x, mix_nsa, mix_sb, mix_moba, w_out, layer, tm):
    m, d = x.shape
    return pl.pallas_call(
        _out_proj_kernel,
        grid=(m // tm, d // COL_TILE),
        in_specs=[
            pl.BlockSpec((tm, COL_TILE), lambda i, j: (i, j)),
            pl.BlockSpec((tm, mix_nsa.shape[1]), lambda i, j: (i, 0)),
            pl.BlockSpec((tm, mix_sb.shape[1]), lambda i, j: (i, 0)),
            pl.BlockSpec((tm, mix_moba.shape[1]), lambda i, j: (i, 0)),
            pl.BlockSpec((None, MIX_WIDTH, COL_TILE), lambda i, j: (layer, 0, j)),
        ],
        out_specs=pl.BlockSpec((tm, COL_TILE), lambda i, j: (i, j)),
        out_shape=jax.ShapeDtypeStruct((m, d), F32),
        compiler_params=_cparams(("arbitrary", "arbitrary")),
        name="out_proj",
    )(x, mix_nsa, mix_sb, mix_moba, w_out)


def _ffn_kernel(x_ref, g_ref, wu_ref, wd_ref, gf_ref, o_ref, h_ref, *, final_norm):
    j = pl.program_id(1)

    @pl.when(j == 0)
    def _():
        x = x_ref[...]
        h_ref[...] = _rms(x, g_ref[...]).astype(BF16)
        o_ref[...] = x

    a = jnp.maximum(_dot(h_ref[...], wu_ref[...]), 0.0)
    o_ref[...] += _dot((a * a).astype(BF16), wd_ref[...])

    if final_norm:
        @pl.when(j == pl.num_programs(1) - 1)
        def _():
            o_ref[...] = _rms(o_ref[...], gf_ref[...])


def _ffn(x, g, w_up, w_down, g_final, layer, tm, final_norm):
    m, d = x.shape
    d_ff = w_up.shape[2]
    return pl.pallas_call(
        functools.partial(_ffn_kernel, final_norm=final_norm),
        grid=(m // tm, d_ff // COL_TILE),
        in_specs=[
            pl.BlockSpec((tm, d), lambda i, j: (i, 0)),
            pl.BlockSpec((None, 1, d), lambda i, j: (layer, 0, 0)),
            pl.BlockSpec((None, d, COL_TILE), lambda i, j: (layer, 0, j)),
            pl.BlockSpec((None, COL_TILE, d), lambda i, j: (layer, j, 0)),
            pl.BlockSpec((1, d), lambda i, j: (0, 0)),
        ],
        out_specs=pl.BlockSpec((tm, d), lambda i, j: (i, 0)),
        out_shape=jax.ShapeDtypeStruct((m, d), F32),
        scratch_shapes=[pltpu.VMEM((tm, d), BF16)],
        compiler_params=_cparams(("arbitrary", "arbitrary")),
        name="ffn",
    )(x, g, w_up, w_down, g_final)


def _online_step(m_s, l_s, acc_s, slot, s, mask, v):
    if mask is not None:
        s = jnp.where(mask, s, NEG)
    m_prev = m_s[slot]
    m_new = jnp.maximum(m_prev, jnp.max(s, axis=-1, keepdims=True))
    alpha = jnp.exp(m_prev - m_new)
    p = jnp.exp(s - m_new)
    l_s[slot] = alpha * l_s[slot] + jnp.sum(p, axis=-1, keepdims=True)
    acc_s[slot] = alpha * acc_s[slot] + _dot(p.astype(BF16), v)
    m_s[slot] = m_new


def _reset_state(m_s, l_s, acc_s):
    m_s[...] = jnp.full(m_s.shape, NEG, F32)
    l_s[...] = jnp.zeros(l_s.shape, F32)
    acc_s[...] = jnp.zeros(acc_s.shape, F32)


def _tile(ref, j, t):
    return ref[pl.ds(pl.multiple_of(j * t, t), t), :]


def _sb_prompt_kernel(q_ref, k_ref, v_ref, o_ref):
    t = ATT_TILE
    i = pl.program_id(2)
    q = q_ref[...]
    r = _iota((t, t), 0)
    c = _iota((t, t), 1)
    u_incl = (r >= c).astype(BF16)
    strict = r > c

    def step(j, carry, acc, mask):
        z = _dot_nt(q, _tile(k_ref, j, t))
        sp = _softplus(z)
        lk = -sp if mask is None else jnp.where(mask, -sp, 0.0)
        incl = _dot_hilo(lk, u_incl)
        logit = z - sp + (carry + incl - lk)
        a = jnp.exp(logit)
        if mask is not None:
            a = jnp.where(mask, a, 0.0)
        acc = acc + _dot(a.astype(BF16), _tile(v_ref, j, t))
        return carry + incl[:, 0:1], acc

    carry, acc = step(i, jnp.zeros((t, 1), F32), jnp.zeros((t, HEAD_DIM), F32), strict)

    def body(d, state):
        return step(i - d, state[0], state[1], None)

    _, acc = lax.fori_loop(1, i + 1, body, (carry, acc))
    o_ref[...] = acc.astype(o_ref.dtype)


def _sb_prompt(q, kvb, batch, seq):
    nq = seq // ATT_TILE
    kv0 = (NSA_ROW_COLS + WIN_ROW_COLS) // HEAD_DIM
    q0 = NSA_HEADS
    return pl.pallas_call(
        _sb_prompt_kernel,
        grid=(batch, SB_HEADS, nq),
        in_specs=[
            pl.BlockSpec((ATT_TILE, HEAD_DIM), lambda b, h, i: (b * nq + i, q0 + h)),
            pl.BlockSpec((seq, HEAD_DIM), lambda b, h, i: (b, kv0 + h)),
            pl.BlockSpec((seq, HEAD_DIM), lambda b, h, i: (b, kv0 + SB_HEADS + h)),
        ],
        out_specs=pl.BlockSpec((ATT_TILE, HEAD_DIM), lambda b, h, i: (b * nq + i, h)),
        out_shape=jax.ShapeDtypeStruct((batch * seq, SB_HEADS * HEAD_DIM), BF16),
        compiler_params=_cparams(("arbitrary", "arbitrary", "arbitrary")),
        name="sb_prompt",
    )(q, kvb, kvb)


def _moba_prompt_kernel(q_ref, k_ref, v_ref, bias_ref, o_ref, kmean_s, m_s, l_s, acc_s):
    t = ATT_TILE
    i = pl.program_id(2)
    seq = k_ref.shape[0]
    nb = seq // t

    @pl.when(i == 0)
    def _():
        avg = jnp.where(_iota((nb, seq), 1) // t == _iota((nb, seq), 0), 1.0 / t, 0.0).astype(BF16)
        kmean_s[...] = _dot(avg, k_ref[...])

    q = q_ref[...]
    km = kmean_s[...]
    km_hi = km.astype(BF16)
    km_lo = (km - km_hi.astype(F32)).astype(BF16)
    gs = _dot_nt(q, km_hi) + _dot_nt(q, km_lo)
    ncol = _iota((1, nb), 1)
    past = ncol < i
    score = jnp.where(past, gs, NEG)
    rank = jnp.zeros((t, nb), F32)
    for mm in range(nb):
        sm = score[:, mm:mm + 1]
        first = (mm < ncol).astype(F32)
        rank += jnp.where(sm > score, 1.0, jnp.where(sm == score, first, 0.0))
    sel = jnp.where(past & (rank < MOBA_TOPK), 1.0, 0.0)

    _reset_state(m_s, l_s, acc_s)
    causal = _iota((t, t), 0) >= _iota((t, t), 1)
    s = _dot_nt(q, _tile(k_ref, i, t)) + bias_ref[0]
    _online_step(m_s, l_s, acc_s, 0, s, causal, _tile(v_ref, i, t))

    def body(d, _):
        j = i - d
        picked = jnp.sum(jnp.where(ncol == j, sel, 0.0), axis=-1, keepdims=True) > 0.5
        s = _dot_nt(q, _tile(k_ref, j, t)) + bias_ref[d]
        _online_step(m_s, l_s, acc_s, 0, s, picked, _tile(v_ref, j, t))
        return 0

    lax.fori_loop(1, i + 1, body, 0)
    o_ref[...] = (acc_s[0] / l_s[0]).astype(o_ref.dtype)


def _moba_prompt(q, kvb, bias_tiles, batch, seq):
    nq = seq // ATT_TILE
    kv0 = (NSA_ROW_COLS + WIN_ROW_COLS + SB_ROW_COLS) // HEAD_DIM
    q0 = NSA_HEADS + SB_HEADS
    nd = bias_tiles.shape[1]
    return pl.pallas_call(
        _moba_prompt_kernel,
        grid=(batch, MOBA_HEADS, nq),
        in_specs=[
            pl.BlockSpec((ATT_TILE, HEAD_DIM), lambda b, h, i: (b * nq + i, q0 + h)),
            pl.BlockSpec((seq, HEAD_DIM), lambda b, h, i: (b, kv0 + h)),
            pl.BlockSpec((seq, HEAD_DIM), lambda b, h, i: (b, kv0 + MOBA_HEADS + h)),
            pl.BlockSpec((None, nd, ATT_TILE, ATT_TILE), lambda b, h, i: (NSA_HEADS + h, 0, 0, 0)),
        ],
        out_specs=pl.BlockSpec((ATT_TILE, HEAD_DIM), lambda b, h, i: (b * nq + i, h)),
        out_shape=jax.ShapeDtypeStruct((batch * seq, MOBA_HEADS * HEAD_DIM), BF16),
        scratch_shapes=[
            pltpu.VMEM((seq // MOBA_BLOCK, HEAD_DIM), F32),
            pltpu.VMEM((1, ATT_TILE, 1), F32),
            pltpu.VMEM((1, ATT_TILE, 1), F32),
            pltpu.VMEM((1, ATT_TILE, HEAD_DIM), F32),
        ],
        compiler_params=_cparams(("arbitrary", "arbitrary", "arbitrary")),
        name="moba_prompt",
    )(q, kvb, kvb, bias_tiles)


def _compress_tokens(x, pe, alpha_b, w1, w2):
    z = jnp.sum((x + pe[None]) * alpha_b[None], axis=1)
    hid = _dot(z.astype(BF16), w1.astype(BF16))
    hid = hid * jax.nn.sigmoid(hid)
    return _dot(hid.astype(BF16), w2.astype(BF16))


def _nsa_prompt_kernel(q_ref, ck_ref, cv_ref, sk_ref, sv_ref, wk_ref, wv_ref, alpha_ref, pe_ref,
                       w1_ref, w2_ref, biasc_ref, bias_ref, gate_ref, o_ref,
                       kc_s, vc_s, ocmp_s, m_s, l_s, acc_s):
    t = ATT_TILE
    i = pl.program_id(2)
    seq = ck_ref.shape[0]
    nb = seq // NSA_BLOCK
    per_tile = t // NSA_BLOCK

    @pl.when(i == 0)
    def _():
        for c, (src, dst) in enumerate(((ck_ref, kc_s), (cv_ref, vc_s))):
            x = src[...].reshape(nb, NSA_BLOCK, HEAD_DIM)
            dst[...] = _compress_tokens(x, pe_ref[c], alpha_ref[c], w1_ref[c], w2_ref[c]).astype(BF16)

    tpos = i * t + _iota((t, 1), 0)
    ncol = _iota((1, nb), 1)
    complete = tpos >= ncol * NSA_BLOCK + (NSA_BLOCK - 1)
    kc = kc_s[...]
    vc = vc_s[...]
    imp = jnp.zeros((t, nb), F32)
    for jh in range(NSA_GROUP):
        qh = q_ref[:, jh * HEAD_DIM:(jh + 1) * HEAD_DIM]
        z = jnp.where(complete, _dot_nt(qh, kc) + biasc_ref[jh], NEG)
        z = z - jnp.max(z, axis=-1, keepdims=True)
        e = jnp.where(complete, jnp.exp(z), 0.0)
        p = e / jnp.maximum(jnp.sum(e, axis=-1, keepdims=True), 1e-30)
        imp += p
        ocmp_s[jh] = _dot(p.astype(BF16), vc)

    forced = (ncol == tpos // NSA_BLOCK) | (ncol == 0)
    score = jnp.where(forced, FORCED, jnp.where(complete, imp, NEG))
    rank = jnp.zeros((t, nb), F32)
    for mm in range(nb):
        sm = score[:, mm:mm + 1]
        first = (mm < ncol).astype(F32)
        rank += jnp.where(sm > score, 1.0, jnp.where(sm == score, first, 0.0))
    sel = jnp.where((rank < NSA_N_SEL) & (score > 0.5 * NEG), 1.0, 0.0).astype(BF16)

    def key_mask(j):
        expand = (_iota((nb, t), 0) == j * per_tile + _iota((nb, t), 1) // NSA_BLOCK).astype(BF16)
        return _dot(sel, expand) > 0.5

    _reset_state(m_s, l_s, acc_s)
    rr = _iota((t, t), 0)
    cc = _iota((t, t), 1)
    causal = rr >= cc

    def tile_step(d, with_window):
        j = i - d
        sk = _tile(sk_ref, j, t)
        sv = _tile(sv_ref, j, t)
        slc_mask = key_mask(j)
        if with_window:
            wk = _tile(wk_ref, j, t)
            wv = _tile(wv_ref, j, t)
            dist = d * t + rr - cc
            win_mask = (dist >= 0) & (dist < NSA_WINDOW)
            slc_mask = slc_mask & (dist >= 0)
        for jh in range(NSA_GROUP):
            qh = q_ref[:, jh * HEAD_DIM:(jh + 1) * HEAD_DIM]
            bias = bias_ref[jh, d]
            _online_step(m_s, l_s, acc_s, jh, _dot_nt(qh, sk) + bias, slc_mask, sv)
            if with_window:
                _online_step(m_s, l_s, acc_s, NSA_GROUP + jh, _dot_nt(qh, wk) + bias, win_mask, wv)

    tile_step(0, True)
    n_win = NSA_WINDOW // t

    def body_win(d, _):
        tile_step(d, True)
        return 0

    def body_far(d, _):
        tile_step(d, False)
        return 0

    lax.fori_loop(1, jnp.minimum(i, n_win) + 1, body_win, 0)
    lax.fori_loop(n_win + 1, i + 1, body_far, 0)

    gates = gate_ref[...]
    for jh in range(NSA_GROUP):
        g_cmp = gates[:, jh:jh + 1]
        g_slc = gates[:, NSA_GROUP + jh:NSA_GROUP + jh + 1]
        g_win = gates[:, 2 * NSA_GROUP + jh:2 * NSA_GROUP + jh + 1]
        o = (g_cmp * ocmp_s[jh] + g_slc * (acc_s[jh] / l_s[jh])
             + g_win * (acc_s[NSA_GROUP + jh] / l_s[NSA_GROUP + jh]))
        o_ref[:, jh * HEAD_DIM:(jh + 1) * HEAD_DIM] = o.astype(o_ref.dtype)


def _nsa_prompt(q, nsa_rows, kvb, gates, alpha_b, pe, w1, w2, biasc, bias_tiles, layer, batch, seq):
    nq = seq // ATT_TILE
    nb = seq // NSA_BLOCK
    nd = bias_tiles.shape[1]
    gw = NSA_GROUP * HEAD_DIM
    win0 = NSA_ROW_COLS // HEAD_DIM

    def col(c):
        return lambda b, g, i: (b, c + g)

    return pl.pallas_call(
        _nsa_prompt_kernel,
        grid=(batch, NSA_KV, nq),
        in_specs=[
            pl.BlockSpec((ATT_TILE, gw), lambda b, g, i: (b * nq + i, g)),
            pl.BlockSpec((seq, HEAD_DIM), col(0)),
            pl.BlockSpec((seq, HEAD_DIM), col(NSA_KV)),
            pl.BlockSpec((seq, HEAD_DIM), col(2 * NSA_KV)),
            pl.BlockSpec((seq, HEAD_DIM), col(3 * NSA_KV)),
            pl.BlockSpec((seq, HEAD_DIM), col(win0)),
            pl.BlockSpec((seq, HEAD_DIM), col(win0 + NSA_KV)),
            pl.BlockSpec((None, 2, NSA_BLOCK, HEAD_DIM), lambda b, g, i: (layer, 0, 0, 0)),
            pl.BlockSpec((None, 2, NSA_BLOCK, HEAD_DIM), lambda b, g, i: (layer, 0, 0, 0)),
            pl.BlockSpec((None, 2, HEAD_DIM, HEAD_DIM), lambda b, g, i: (layer, 0, 0, 0)),
            pl.BlockSpec((None, 2, HEAD_DIM, HEAD_DIM), lambda b, g, i: (layer, 0, 0, 0)),
            pl.BlockSpec((None, NSA_GROUP, ATT_TILE, nb), lambda b, g, i: (g, 0, i, 0)),
            pl.BlockSpec((NSA_GROUP, nd, ATT_TILE, ATT_TILE), lambda b, g, i: (g, 0, 0, 0)),
            pl.BlockSpec((ATT_TILE, LANES), lambda b, g, i: (b * nq + i, g)),
        ],
        out_specs=pl.BlockSpec((ATT_TILE, gw), lambda b, g, i: (b * nq + i, g)),
        out_shape=jax.ShapeDtypeStruct((batch * seq, NSA_HEADS * HEAD_DIM), BF16),
        scratch_shapes=[
            pltpu.VMEM((nb, HEAD_DIM), BF16),
            pltpu.VMEM((nb, HEAD_DIM), BF16),
            pltpu.VMEM((NSA_GROUP, ATT_TILE, HEAD_DIM), F32),
            pltpu.VMEM((2 * NSA_GROUP, ATT_TILE, 1), F32),
            pltpu.VMEM((2 * NSA_GROUP, ATT_TILE, 1), F32),
            pltpu.VMEM((2 * NSA_GROUP, ATT_TILE, HEAD_DIM), F32),
        ],
        compiler_params=_cparams(("arbitrary", "arbitrary", "arbitrary")),
        name="nsa_prompt",
    )(q, nsa_rows, nsa_rows, kvb, kvb, kvb, kvb, alpha_b, pe, w1, w2, biasc, bias_tiles, gates)


SB_PAGES_PER_STEP = 8
MOBA_PAGES_PER_STEP = 16
NSA_PAGES_PER_STEP = 4
NSA_BLOCKS_PER_STEP = 5


def _sb_sample_kernel(pt_ref, q_ref, *refs):
    del pt_ref
    pages = refs[:SB_PAGES_PER_STEP]
    o_ref, carry_s, acc_s = refs[SB_PAGES_PER_STEP:]
    s = pl.program_id(1)
    hw = SB_HEADS * HEAD_DIM
    page = pages[0].shape[0]

    @pl.when(s == 0)
    def _():
        carry_s[...] = jnp.zeros(carry_s.shape, F32)
        acc_s[...] = jnp.zeros(acc_s.shape, F32)

    own = _iota((8, hw), 1) // HEAD_DIM == _iota((8, hw), 0)
    q_bd = jnp.where(own, q_ref[...], 0.0).astype(BF16)
    u_incl = (_iota((page, page), 0) >= _iota((page, page), 1)).astype(BF16)
    carry = carry_s[...]
    acc = acc_s[...]
    for ref in pages:
        k = ref[:, 0:hw].astype(BF16)
        v = ref[:, hw:2 * hw].astype(BF16)
        z = _dot_nt(q_bd, k)
        sp = _softplus(z)
        lk = -sp
        incl = _dot_hilo(lk, u_incl)
        a = jnp.exp(z - sp + (carry + incl - lk))
        acc = acc + _dot(a.astype(BF16), v)
        carry = carry + incl[:, 0:1]
    carry_s[...] = carry
    acc_s[...] = acc

    @pl.when(s == pl.num_programs(1) - 1)
    def _():
        o_ref[...] = jnp.sum(jnp.where(own, acc, 0.0), axis=0, keepdims=True)


def _sb_sample(page_table, q3, cache, layer):
    batch, n_pages = page_table.shape
    page = cache.shape[2]
    hw = SB_HEADS * HEAD_DIM
    steps = n_pages // SB_PAGES_PER_STEP

    def page_map(k):
        return lambda b, s, pt: (layer, pt[b, n_pages - 1 - (s * SB_PAGES_PER_STEP + k)], 0, 0)

    grid_spec = pltpu.PrefetchScalarGridSpec(
        num_scalar_prefetch=1,
        grid=(batch, steps),
        in_specs=[pl.BlockSpec((None, 1, hw), lambda b, s, pt: (b, 0, NSA_HEADS * HEAD_DIM // hw))]
        + [pl.BlockSpec((None, None, page, 2 * hw), page_map(k)) for k in range(SB_PAGES_PER_STEP)],
        out_specs=pl.BlockSpec((None, 1, hw), lambda b, s, pt: (b, 0, 0)),
        scratch_shapes=[pltpu.VMEM((8, 1), F32), pltpu.VMEM((8, hw), F32)],
    )
    return pl.pallas_call(
        _sb_sample_kernel,
        grid_spec=grid_spec,
        out_shape=jax.ShapeDtypeStruct((batch, 1, hw), F32),
        compiler_params=_cparams(("arbitrary", "arbitrary")),
        name="sb_sample",
    )(page_table, q3, *([cache] * SB_PAGES_PER_STEP))


def _moba_select_kernel(pt_ref, q_ref, *refs):
    del pt_ref
    pages = refs[:MOBA_PAGES_PER_STEP]
    idx_ref, kmean_s = refs[MOBA_PAGES_PER_STEP:]
    s = pl.program_id(1)
    hw = MOBA_HEADS * HEAD_DIM
    page = pages[0].shape[0]
    per_block = MOBA_BLOCK // page
    rows = MOBA_PAGES_PER_STEP // per_block
    nb = kmean_s.shape[0]

    sums = [jnp.sum(ref[...], axis=0, keepdims=True) for ref in pages]
    means = [sum(sums[r * per_block:(r + 1) * per_block]) * (1.0 / MOBA_BLOCK) for r in range(rows)]
    kmean_s[pl.ds(pl.multiple_of(s * rows, rows), rows), :] = jnp.concatenate(means, axis=0)

    @pl.when(s == pl.num_programs(1) - 1)
    def _():
        prod = kmean_s[...] * q_ref[...]
        seg = (_iota((hw, LANES), 0) // HEAD_DIM == _iota((hw, LANES), 1)).astype(BF16)
        hi = prod.astype(BF16)
        mid = (prod - hi.astype(F32))
        mid_b = mid.astype(BF16)
        lo = (mid - mid_b.astype(F32)).astype(BF16)
        gs = _dot(hi, seg) + _dot(mid_b, seg) + _dot(lo, seg)
        nrow = _iota((nb, 1), 0)
        rank = jnp.zeros((nb, LANES), F32)
        for mm in range(nb):
            gm = gs[mm:mm + 1, :]
            first = (mm < nrow).astype(F32)
            rank += jnp.where(gm > gs, 1.0, jnp.where(gm == gs, first, 0.0))
        nrow_f = nrow.astype(F32)
        picks = [jnp.sum(jnp.where(rank == float(k), nrow_f, 0.0), axis=0, keepdims=True) for k in range(MOBA_TOPK)]
        picks += [jnp.zeros((1, LANES), F32)] * (8 - MOBA_TOPK)
        idx_ref[...] = jnp.concatenate(picks, axis=0).astype(I32)


def _moba_select(page_table, q3, cache, layer):
    batch, n_pages = page_table.shape
    page = cache.shape[2]
    hw = MOBA_HEADS * HEAD_DIM
    steps = n_pages // MOBA_PAGES_PER_STEP
    nb = n_pages * page // MOBA_BLOCK

    def page_map(k):
        return lambda b, s, pt: (layer, pt[b, s * MOBA_PAGES_PER_STEP + k], 0, 0)

    grid_spec = pltpu.PrefetchScalarGridSpec(
        num_scalar_prefetch=1,
        grid=(batch, steps),
        in_specs=[pl.BlockSpec((None, 1, hw), lambda b, s, pt: (b, 0, (NSA_HEADS + SB_HEADS) * HEAD_DIM // hw))]
        + [pl.BlockSpec((None, None, page, hw), page_map(k)) for k in range(MOBA_PAGES_PER_STEP)],
        out_specs=pl.BlockSpec((None, 8, LANES), lambda b, s, pt: (b, 0, 0)),
        scratch_shapes=[pltpu.VMEM((nb, hw), F32)],
    )
    return pl.pallas_call(
        _moba_select_kernel,
        grid_spec=grid_spec,
        out_shape=jax.ShapeDtypeStruct((batch, 8, LANES), I32),
        compiler_params=_cparams(("arbitrary", "arbitrary")),
        name="moba_select",
    )(page_table, q3, *([cache] * MOBA_PAGES_PER_STEP))


def _moba_sample_kernel(pt_ref, idx_ref, q_ref, k_ref, v_ref, bias_ref, knew_ref, vnew_ref, bias0_ref,
                        o_ref, m_s, l_s, acc_s):
    del pt_ref, idx_ref
    kk = pl.program_id(2)
    q = q_ref[...]

    @pl.when(kk == 0)
    def _():
        m_s[...] = jnp.sum(q * knew_ref[...], axis=-1, keepdims=True) + bias0_ref[:, 0:1]
        l_s[...] = jnp.ones(l_s.shape, F32)
        acc_s[...] = vnew_ref[...]

    q8 = jnp.broadcast_to(q, (8, HEAD_DIM)).astype(BF16)
    s = (_dot_nt(q8, k_ref[...].astype(BF16)) + bias_ref[...])[0:1, :]
    m_prev = m_s[...]
    m_new = jnp.maximum(m_prev, jnp.max(s, axis=-1, keepdims=True))
    alpha = jnp.exp(m_prev - m_new)
    p = jnp.exp(s - m_new)
    l_s[...] = alpha * l_s[...] + jnp.sum(p, axis=-1, keepdims=True)
    p8 = jnp.broadcast_to(p, (8, p.shape[1])).astype(BF16)
    acc_s[...] = alpha * acc_s[...] + _dot(p8, v_ref[...].astype(BF16))[0:1, :]
    m_s[...] = m_new

    @pl.when(kk == pl.num_programs(2) - 1)
    def _():
        o_ref[...] = acc_s[...] / l_s[...]


def _moba_sample(page_table, idx, q3, cache, rows3, bias_pages, bias0, layer):
    batch, n_pages = page_table.shape
    page = cache.shape[2]
    per_block = MOBA_BLOCK // page
    steps = MOBA_TOPK * per_block

    def pg(b, h, kk, idx_ref):
        return idx_ref[b, kk // per_block, h] * per_block + kk % per_block

    grid_spec = pltpu.PrefetchScalarGridSpec(
        num_scalar_prefetch=2,
        grid=(batch, MOBA_HEADS, steps),
        in_specs=[
            pl.BlockSpec((None, 1, HEAD_DIM), lambda b, h, kk, pt, ix: (b, 0, NSA_HEADS + SB_HEADS + h)),
            pl.BlockSpec((None, None, page, HEAD_DIM), lambda b, h, kk, pt, ix: (layer, pt[b, pg(b, h, kk, ix)], 0, h)),
            pl.BlockSpec((None, None, page, HEAD_DIM),
                         lambda b, h, kk, pt, ix: (layer, pt[b, pg(b, h, kk, ix)], 0, MOBA_HEADS + h)),
            pl.BlockSpec((None, None, 1, page), lambda b, h, kk, pt, ix: (h, pg(b, h, kk, ix), 0, 0)),
            pl.BlockSpec((None, 1, HEAD_DIM), lambda b, h, kk, pt, ix: (b, 0, h)),
            pl.BlockSpec((None, 1, HEAD_DIM), lambda b, h, kk, pt, ix: (b, 0, MOBA_HEADS + h)),
            pl.BlockSpec((None, 1, LANES), lambda b, h, kk, pt, ix: (h, 0, 0)),
        ],
        out_specs=pl.BlockSpec((None, 1, HEAD_DIM), lambda b, h, kk, pt, ix: (b, 0, h)),
        scratch_shapes=[pltpu.VMEM((1, 1), F32), pltpu.VMEM((1, 1), F32), pltpu.VMEM((1, HEAD_DIM), F32)],
    )
    return pl.pallas_call(
        _moba_sample_kernel,
        grid_spec=grid_spec,
        out_shape=jax.ShapeDtypeStruct((batch, 1, MOBA_HEADS * HEAD_DIM), F32),
        compiler_params=_cparams(("arbitrary", "arbitrary", "arbitrary")),
        name="moba_sample",
    )(page_table, idx, q3, cache, cache, bias_pages, rows3, rows3, bias0)


def _nsa_select_kernel(pt_ref, qg_ref, alpha_ref, pe_ref, w1_ref, w2_ref, biasc_ref, *refs):
    del pt_ref
    pages = refs[:NSA_PAGES_PER_STEP]
    ocmp_ref, idx_ref, z_s = refs[NSA_PAGES_PER_STEP:]
    s = pl.program_id(1)
    page = pages[0].shape[0]
    per_page = page // NSA_BLOCK
    rows = NSA_PAGES_PER_STEP * per_page
    nb = z_s.shape[0]
    cw = 2 * NSA_KV * HEAD_DIM

    alpha = alpha_ref[...]
    pe = pe_ref[...]
    zs = [jnp.sum((ref[...].reshape(per_page, NSA_BLOCK, cw) + pe[None]) * alpha[None], axis=1) for ref in pages]
    z_s[pl.ds(pl.multiple_of(s * rows, rows), rows), :] = jnp.concatenate(zs, axis=0)

    @pl.when(s == pl.num_programs(1) - 1)
    def _():
        z = z_s[...]
        n_lane = _iota((1, nb), 1)
        for g in range(NSA_KV):
            toks = []
            for c in range(2):
                zz = z[:, (c * NSA_KV + g) * HEAD_DIM:(c * NSA_KV + g + 1) * HEAD_DIM]
                hid = _dot(zz.astype(BF16), w1_ref[c].astype(BF16))
                hid = hid * jax.nn.sigmoid(hid)
                toks.append(_dot(hid.astype(BF16), w2_ref[c].astype(BF16)).astype(BF16))
            kc, vc = toks
            lc = _dot_nt(qg_ref[g].astype(BF16), kc) + biasc_ref[g]
            lc = lc - jnp.max(lc, axis=-1, keepdims=True)
            e = jnp.exp(lc)
            p = e / jnp.maximum(jnp.sum(e, axis=-1, keepdims=True), 1e-30)
            ocmp_ref[g] = _dot(p.astype(BF16), vc)
            imp = jnp.sum(p[0:NSA_GROUP], axis=0, keepdims=True)
            by_lane = jnp.broadcast_to(imp, (nb, nb))
            by_row = by_lane.T
            m_idx = _iota((nb, nb), 0)
            n_idx = _iota((nb, nb), 1)
            beats = jnp.where(by_row > by_lane, 1.0,
                              jnp.where((by_row == by_lane) & (m_idx < n_idx), 1.0, 0.0))
            beats = jnp.where(m_idx >= 1, beats, 0.0)
            rank = jnp.sum(beats, axis=0, keepdims=True)
            k_idx = _iota((NSA_N_SEL, nb), 0)
            hit = ((jnp.broadcast_to(rank, (NSA_N_SEL, nb)) == k_idx.astype(F32)) & (n_lane >= 1)
                   & (k_idx < NSA_N_SEL - 2))
            pick = jnp.sum(jnp.where(hit, n_lane.astype(F32), 0.0), axis=-1, keepdims=True)
            idx_ref[g] = jnp.broadcast_to(pick, (NSA_N_SEL, LANES)).astype(I32)


def _nsa_select(page_table, qg, cache, alpha2, pe2, w1, w2, biasc_s, layer):
    batch, n_pages = page_table.shape
    page = cache.shape[2]
    cw = 2 * NSA_KV * HEAD_DIM
    steps = n_pages // NSA_PAGES_PER_STEP
    nb = n_pages * page // NSA_BLOCK

    def page_map(k):
        return lambda b, s, pt: (layer, pt[b, s * NSA_PAGES_PER_STEP + k], 0, 0)

    grid_spec = pltpu.PrefetchScalarGridSpec(
        num_scalar_prefetch=1,
        grid=(batch, steps),
        in_specs=[
            pl.BlockSpec((None, NSA_KV, 8, HEAD_DIM), lambda b, s, pt: (b, 0, 0, 0)),
            pl.BlockSpec((None, NSA_BLOCK, cw), lambda b, s, pt: (layer, 0, 0)),
            pl.BlockSpec((None, NSA_BLOCK, cw), lambda b, s, pt: (layer, 0, 0)),
            pl.BlockSpec((None, 2, HEAD_DIM, HEAD_DIM), lambda b, s, pt: (layer, 0, 0, 0)),
            pl.BlockSpec((None, 2, HEAD_DIM, HEAD_DIM), lambda b, s, pt: (layer, 0, 0, 0)),
            pl.BlockSpec((NSA_KV, 8, nb), lambda b, s, pt: (0, 0, 0)),
        ] + [pl.BlockSpec((None, None, page, cw), page_map(k)) for k in range(NSA_PAGES_PER_STEP)],
        out_specs=[
            pl.BlockSpec((None, NSA_KV, 8, HEAD_DIM), lambda b, s, pt: (b, 0, 0, 0)),
            pl.BlockSpec((None, NSA_KV, NSA_N_SEL, LANES), lambda b, s, pt: (b, 0, 0, 0)),
        ],
        scratch_shapes=[pltpu.VMEM((nb, cw), F32)],
    )
    return pl.pallas_call(
        _nsa_select_kernel,
        grid_spec=grid_spec,
        out_shape=[jax.ShapeDtypeStruct((batch, NSA_KV, 8, HEAD_DIM), F32),
                   jax.ShapeDtypeStruct((batch, NSA_KV, NSA_N_SEL, LANES), I32)],
        compiler_params=_cparams(("arbitrary", "arbitrary")),
        name="nsa_select",
    )(page_table, qg, alpha2, pe2, w1, w2, biasc_s, *([cache] * NSA_PAGES_PER_STEP))


def _nsa_sample_kernel(pt_ref, idx_ref, qg_ref, ocmp_ref, gate_ref, sknew_ref, svnew_ref, wknew_ref, wvnew_ref,
                       wk_ref, wv_ref, biasw_ref, bias0_ref, *refs):
    del pt_ref, idx_ref
    nblk = NSA_BLOCKS_PER_STEP
    k_refs = refs[0:nblk]
    v_refs = refs[nblk:2 * nblk]
    b_refs = refs[2 * nblk:3 * nblk]
    o_ref, m_s, l_s, acc_s = refs[3 * nblk:]
    kk = pl.program_id(2)
    qf = qg_ref[...]
    q = qf.astype(BF16)
    bias0 = bias0_ref[:, 0:1]

    @pl.when(kk == 0)
    def _():
        m_s[...] = jnp.sum(qf * sknew_ref[...], axis=-1, keepdims=True) + bias0
        l_s[...] = jnp.ones(l_s.shape, F32)
        acc_s[...] = jnp.broadcast_to(svnew_ref[...], acc_s.shape)

    m = m_s[...]
    l = l_s[...]
    acc = acc_s[...]
    for k_ref, v_ref, b_ref in zip(k_refs, v_refs, b_refs):
        s = _dot_nt(q, k_ref[...].astype(BF16)) + b_ref[...]
        m_new = jnp.maximum(m, jnp.max(s, axis=-1, keepdims=True))
        alpha = jnp.exp(m - m_new)
        p = jnp.exp(s - m_new)
        l = alpha * l + jnp.sum(p, axis=-1, keepdims=True)
        acc = alpha * acc + _dot(p.astype(BF16), v_ref[...].astype(BF16))
        m = m_new
    m_s[...] = m
    l_s[...] = l
    acc_s[...] = acc

    @pl.when(kk == pl.num_programs(2) - 1)
    def _():
        o_slc = acc / l
        w = wk_ref.shape[0]
        sw = _dot_nt(q, wk_ref[...].astype(BF16)) + biasw_ref[...]
        valid = _iota((1, w), 1) >= 1
        sw = jnp.where(valid, sw, NEG)
        s_new = jnp.sum(qf * wknew_ref[...], axis=-1, keepdims=True) + bias0
        mw = jnp.maximum(jnp.max(sw, axis=-1, keepdims=True), s_new)
        pw = jnp.where(valid, jnp.exp(sw - mw), 0.0)
        p_new = jnp.exp(s_new - mw)
        den = jnp.sum(pw, axis=-1, keepdims=True) + p_new
        o_win = (_dot(pw.astype(BF16), wv_ref[...].astype(BF16)) + p_new * wvnew_ref[...]) / den
        gates = jnp.broadcast_to(gate_ref[...], (8, LANES))
        lane = _iota((8, LANES), 1)
        row = _iota((8, LANES), 0)

        def gate(c):
            return jnp.sum(jnp.where(lane == c * NSA_GROUP + row, gates, 0.0), axis=-1, keepdims=True)

        o = gate(0) * ocmp_ref[...] + gate(1) * o_slc + gate(2) * o_win
        for jh in range(NSA_GROUP):
            o_ref[:, jh * HEAD_DIM:(jh + 1) * HEAD_DIM] = o[jh:jh + 1, :]


def _nsa_sample(page_table, idx, qg, ocmp, gates3, nsa_rows3, win_rows3, cache2, state, bias_blocks,
                bias_w, bias0, layer, per_page):
    batch, n_pages = page_table.shape
    nblk = NSA_BLOCKS_PER_STEP
    n_cache_blocks = NSA_N_SEL - 1
    steps = n_cache_blocks // nblk
    w = state.shape[2]
    gw = NSA_GROUP * HEAD_DIM

    def blk(b, g, kk, u, ix):
        return ix[b, g, kk * nblk + u]

    def row2(b, g, kk, u, pt, ix):
        n = blk(b, g, kk, u, ix)
        return pt[b, n // per_page] * per_page + n % per_page

    def k_map(u):
        return lambda b, g, kk, pt, ix: (layer, row2(b, g, kk, u, pt, ix), 0, 2 * NSA_KV + g)

    def v_map(u):
        return lambda b, g, kk, pt, ix: (layer, row2(b, g, kk, u, pt, ix), 0, 3 * NSA_KV + g)

    def b_map(u):
        return lambda b, g, kk, pt, ix: (g, blk(b, g, kk, u, ix), 0, 0)

    grid_spec = pltpu.PrefetchScalarGridSpec(
        num_scalar_prefetch=2,
        grid=(batch, NSA_KV, steps),
        in_specs=[
            pl.BlockSpec((None, None, 8, HEAD_DIM), lambda b, g, kk, pt, ix: (b, g, 0, 0)),
            pl.BlockSpec((None, None, 8, HEAD_DIM), lambda b, g, kk, pt, ix: (b, g, 0, 0)),
            pl.BlockSpec((None, 1, LANES), lambda b, g, kk, pt, ix: (b, 0, g)),
            pl.BlockSpec((None, 1, HEAD_DIM), lambda b, g, kk, pt, ix: (b, 0, 2 * NSA_KV + g)),
            pl.BlockSpec((None, 1, HEAD_DIM), lambda b, g, kk, pt, ix: (b, 0, 3 * NSA_KV + g)),
            pl.BlockSpec((None, 1, HEAD_DIM), lambda b, g, kk, pt, ix: (b, 0, g)),
            pl.BlockSpec((None, 1, HEAD_DIM), lambda b, g, kk, pt, ix: (b, 0, NSA_KV + g)),
            pl.BlockSpec((None, None, w, HEAD_DIM), lambda b, g, kk, pt, ix: (layer, b, 0, g)),
            pl.BlockSpec((None, None, w, HEAD_DIM), lambda b, g, kk, pt, ix: (layer, b, 0, NSA_KV + g)),
            pl.BlockSpec((None, 8, w), lambda b, g, kk, pt, ix: (g, 0, 0)),
            pl.BlockSpec((None, 8, LANES), lambda b, g, kk, pt, ix: (g, 0, 0)),
        ]
        + [pl.BlockSpec((None, None, NSA_BLOCK, HEAD_DIM), k_map(u)) for u in range(nblk)]
        + [pl.BlockSpec((None, None, NSA_BLOCK, HEAD_DIM), v_map(u)) for u in range(nblk)]
        + [pl.BlockSpec((None, None, 8, NSA_BLOCK), b_map(u)) for u in range(nblk)],
        out_specs=pl.BlockSpec((None, 1, gw), lambda b, g, kk, pt, ix: (b, 0, g)),
        scratch_shapes=[pltpu.VMEM((8, 1), F32), pltpu.VMEM((8, 1), F32), pltpu.VMEM((8, HEAD_DIM), F32)],
    )
    return pl.pallas_call(
        _nsa_sample_kernel,
        grid_spec=grid_spec,
        out_shape=jax.ShapeDtypeStruct((batch, 1, NSA_HEADS * HEAD_DIM), F32),
        compiler_params=_cparams(("arbitrary", "arbitrary", "arbitrary")),
        name="nsa_sample",
    )(page_table, idx, qg, ocmp, gates3, nsa_rows3, nsa_rows3, win_rows3, win_rows3, state, state,
      bias_w, bias0, *([cache2] * (2 * nblk)), *([bias_blocks] * nblk))


def _rel_bucket(dist):
    exact = REL_BUCKETS // 2
    d = jnp.maximum(dist, 0)
    df = jnp.maximum(d, 1).astype(F32)
    far = exact + (jnp.log(df / exact) / math.log(REL_MAX_DIST / exact) * (REL_BUCKETS - exact)).astype(I32)
    return jnp.where(d < exact, d, jnp.minimum(far, REL_BUCKETS - 1))


def _bias_of(dist, tbl):
    return jnp.moveaxis(tbl[_rel_bucket(dist)], -1, 0)


def _pad_rows(a, axis, n):
    pad = [(0, 0)] * a.ndim
    pad[axis] = (0, n - a.shape[axis])
    return jnp.pad(a, pad)


def kernel(x_prompt, x_sample, cache_nsa, cache_sb, cache_moba, state_nsa_win, page_table, norm_mix, w_in,
           nsa_cmp_pe, nsa_cmp_alpha, nsa_cmp_w1, nsa_cmp_w2, rel_bias, w_out, norm_ffn, w_up, w_down,
           norm_final):
    batch, seq, d = x_prompt.shape
    dec_batch, dec_seq, _ = x_sample.shape
    depth = w_in.shape[0]
    n_pool, page = cache_nsa.shape[1], cache_nsa.shape[2]
    n_pages = page_table.shape[1]
    past = n_pages * page
    w_buf = state_nsa_win.shape[2]
    assert dec_seq == 1 and seq % ATT_TILE == 0 and ATT_TILE == MOBA_BLOCK
    assert page % NSA_BLOCK == 0 and MOBA_BLOCK % page == 0 and w_buf == NSA_WINDOW
    assert past // NSA_BLOCK >= NSA_N_SEL and past // MOBA_BLOCK >= MOBA_TOPK
    assert dec_batch <= SAMPLE_ROWS and (batch * seq) % ROW_TILE == 0

    o_kv = MIX_WIDTH
    o_gate = MIX_WIDTH + KV_COLS
    w_main = w_in[:, :, :o_gate].astype(BF16)
    wg = w_in[:, :, o_gate:].reshape(depth, d, 3, NSA_KV, NSA_GROUP).transpose(0, 1, 3, 2, 4)
    wg = wg.reshape(depth, d, NSA_KV, 3 * NSA_GROUP)
    w_gate = _pad_rows(wg, 3, LANES).reshape(depth, d, NSA_KV * LANES).astype(BF16)
    w_out_b = w_out.astype(BF16)
    w_up_b = w_up.astype(BF16)
    w_down_b = w_down.astype(BF16)
    g_mix = norm_mix.reshape(depth, 1, d)
    g_ffn = norm_ffn.reshape(depth, 1, d)
    g_fin = norm_final.reshape(1, d)
    alpha_b = jnp.broadcast_to(nsa_cmp_alpha[..., None], nsa_cmp_alpha.shape + (HEAD_DIM,))
    alpha2 = jnp.repeat(alpha_b, NSA_KV, axis=1).transpose(0, 2, 1, 3).reshape(depth, NSA_BLOCK, 2 * NSA_KV * HEAD_DIM)
    pe2 = jnp.repeat(nsa_cmp_pe, NSA_KV, axis=1).transpose(0, 2, 1, 3).reshape(depth, NSA_BLOCK, 2 * NSA_KV * HEAD_DIM)

    tbl_nsa = rel_bias[:, :NSA_HEADS]
    tbl_moba = rel_bias[:, NSA_HEADS:]
    nd = seq // ATT_TILE
    tr = jnp.arange(ATT_TILE, dtype=I32)
    dist_tiles = (jnp.arange(nd, dtype=I32)[:, None, None] * ATT_TILE + tr[None, :, None] - tr[None, None, :])
    bias_tiles = _bias_of(dist_tiles, rel_bias)
    nb_p = seq // NSA_BLOCK
    dist_c = jnp.arange(seq, dtype=I32)[:, None] - (jnp.arange(nb_p, dtype=I32) * NSA_BLOCK + NSA_BLOCK - 1)[None, :]
    biasc_p = _bias_of(dist_c, tbl_nsa).reshape(NSA_KV, NSA_GROUP, seq, nb_p)

    nb_s = past // NSA_BLOCK

    def heads8(a):
        a = a.reshape((NSA_KV, NSA_GROUP) + a.shape[1:])
        return _pad_rows(a, 1, 8)

    dist_cs = past - (jnp.arange(nb_s, dtype=I32) * NSA_BLOCK + NSA_BLOCK - 1)
    biasc_s = heads8(_bias_of(dist_cs, tbl_nsa))
    dist_blk = past - (jnp.arange(nb_s, dtype=I32)[:, None] * NSA_BLOCK + jnp.arange(NSA_BLOCK, dtype=I32)[None, :])
    bias_blocks = heads8(_bias_of(dist_blk, tbl_nsa)).transpose(0, 2, 1, 3)
    dist_w = NSA_WINDOW - jnp.arange(w_buf, dtype=I32)
    bias_w = heads8(_bias_of(dist_w, tbl_nsa))
    bias0_nsa = jnp.broadcast_to(heads8(tbl_nsa[0])[..., None], (NSA_KV, 8, LANES))
    dist_pg = past - (jnp.arange(n_pages, dtype=I32)[:, None] * page + jnp.arange(page, dtype=I32)[None, :])
    bias_pages = _bias_of(dist_pg, tbl_moba)[:, :, None, :]
    bias0_moba = jnp.broadcast_to(tbl_moba[0][:, None, None], (MOBA_HEADS, 1, LANES))

    cache_nsa_pg = cache_nsa.reshape(depth, n_pool, page, NSA_ROW_COLS)
    cache_nsa_blk = cache_nsa.reshape(depth, n_pool * (page // NSA_BLOCK), NSA_BLOCK, NSA_ROW_COLS)
    cache_sb_pg = cache_sb.reshape(depth, n_pool, page, SB_ROW_COLS)
    cache_moba_pg = cache_moba.reshape(depth, n_pool, page, MOBA_ROW_COLS)
    state = state_nsa_win.reshape(depth, dec_batch, w_buf, WIN_ROW_COLS)

    xp = x_prompt.reshape(batch * seq, d)
    xs = _pad_rows(x_sample.reshape(dec_batch, d), 0, SAMPLE_ROWS)
    outs = {k: [] for k in ("nsa_p", "nsa_s", "sb_p", "sb_s", "moba_p", "moba_s", "win_p", "win_s")}

    for l in range(depth):
        last = l == depth - 1
        q, nsa_r, win_r, sb_r, moba_r, gates, kvb = _in_proj(xp, g_mix, w_main, w_gate, l, ROW_TILE)
        mix_nsa = _nsa_prompt(q, nsa_r, kvb, gates, alpha_b, nsa_cmp_pe, nsa_cmp_w1, nsa_cmp_w2,
                              biasc_p, bias_tiles, l, batch, seq)
        mix_sb = _sb_prompt(q, kvb, batch, seq)
        mix_moba = _moba_prompt(q, kvb, bias_tiles, batch, seq)
        xp = _out_proj(xp, mix_nsa, mix_sb, mix_moba, w_out_b, l, ROW_TILE)
        xp = _ffn(xp, g_ffn, w_up_b, w_down_b, g_fin, l, ROW_TILE, last)
        outs["nsa_p"].append(nsa_r.reshape(batch, seq, 4, NSA_KV, HEAD_DIM))
        outs["sb_p"].append(sb_r.reshape(batch, seq, 2, SB_HEADS, HEAD_DIM))
        outs["moba_p"].append(moba_r.reshape(batch, seq, 2, MOBA_HEADS, HEAD_DIM))
        win_keep = min(NSA_WINDOW, seq)
        outs["win_p"].append(win_r.reshape(batch, seq, 2, NSA_KV, HEAD_DIM)[:, seq - win_keep:])

        q_s, nsa_s, win_s, sb_s, moba_s, gates_s, _ = _in_proj(xs, g_mix, w_main, w_gate, l, SAMPLE_ROWS)
        q3 = q_s.astype(F32).reshape(SAMPLE_ROWS, 1, MIX_WIDTH)
        qg = _pad_rows(q3[:dec_batch, 0, :NSA_HEADS * HEAD_DIM].reshape(dec_batch, NSA_KV, NSA_GROUP, HEAD_DIM), 2, 8)
        nsa_s3 = nsa_s.reshape(SAMPLE_ROWS, 1, NSA_ROW_COLS)
        win_s3 = win_s.reshape(SAMPLE_ROWS, 1, WIN_ROW_COLS)
        moba_s3 = moba_s.reshape(SAMPLE_ROWS, 1, MOBA_ROW_COLS)
        gates_s3 = gates_s.reshape(SAMPLE_ROWS, 1, 2 * LANES)

        ocmp, sel = _nsa_select(page_table, qg, cache_nsa_pg, alpha2, pe2, nsa_cmp_w1, nsa_cmp_w2, biasc_s, l)
        o_nsa = _nsa_sample(page_table, sel[:, :, :, 0], qg, ocmp, gates_s3, nsa_s3, win_s3, cache_nsa_blk, state,
                            bias_blocks, bias_w, bias0_nsa, l, page // NSA_BLOCK)
        o_sb = _sb_sample(page_table, q3, cache_sb_pg, l)
        top = _moba_select(page_table, q3, cache_moba_pg, l)
        o_moba = _moba_sample(page_table, top[:, :MOBA_TOPK, :MOBA_HEADS], q3, cache_moba_pg, moba_s3,
                              bias_pages, bias0_moba, l)

        def rows16(o):
            return _pad_rows(o.reshape(dec_batch, -1), 0, SAMPLE_ROWS).astype(BF16)

        xs = _out_proj(xs, rows16(o_nsa), rows16(o_sb), rows16(o_moba), w_out_b, l, SAMPLE_ROWS)
        xs = _ffn(xs, g_ffn, w_up_b, w_down_b, g_fin, l, SAMPLE_ROWS, last)
        outs["nsa_s"].append(nsa_s[:dec_batch].reshape(dec_batch, 1, 4, NSA_KV, HEAD_DIM))
        outs["sb_s"].append(sb_s[:dec_batch].reshape(dec_batch, 1, 2, SB_HEADS, HEAD_DIM))
        outs["moba_s"].append(moba_s[:dec_batch].reshape(dec_batch, 1, 2, MOBA_HEADS, HEAD_DIM))
        new_win = win_s[:dec_batch].reshape(dec_batch, 1, 2, NSA_KV, HEAD_DIM)
        outs["win_s"].append(jnp.concatenate([state_nsa_win[l][:, 1:], new_win], axis=1))

    y_prompt = xp.reshape(batch, seq, d)
    y_sample = xs[:dec_batch].reshape(dec_batch, 1, d)
    return (y_prompt, y_sample,
            jnp.stack(outs["nsa_p"]), jnp.stack(outs["nsa_s"]),
            jnp.stack(outs["sb_p"]), jnp.stack(outs["sb_s"]),
            jnp.stack(outs["moba_p"]), jnp.stack(outs["moba_s"]),
            jnp.stack(outs["win_p"]), jnp.stack(outs["win_s"]))
```

```python
import functools
import math

import jax
import jax.numpy as jnp
from jax import lax
from jax.experimental import pallas as pl
from jax.experimental.pallas import tpu as pltpu

F32 = jnp.float32
BF16 = jnp.bfloat16
I32 = jnp.int32

HEAD_DIM = 128
NSA_HEADS = 8
SB_HEADS = 4
MOBA_HEADS = 4
NSA_GROUP = 4
NSA_KV = 2
N_HEADS = NSA_HEADS + SB_HEADS + MOBA_HEADS
MIX_WIDTH = N_HEADS * HEAD_DIM
NSA_SLABS = 4 * NSA_KV
WIN_SLABS = 2 * NSA_KV
SB_SLABS = 2 * SB_HEADS
MOBA_SLABS = 2 * MOBA_HEADS
NSA_ROW_COLS = NSA_SLABS * HEAD_DIM
WIN_ROW_COLS = WIN_SLABS * HEAD_DIM
SB_ROW_COLS = SB_SLABS * HEAD_DIM
MOBA_ROW_COLS = MOBA_SLABS * HEAD_DIM
KV_COLS = NSA_ROW_COLS + WIN_ROW_COLS + SB_ROW_COLS + MOBA_ROW_COLS
NSA_BLOCK = 64
NSA_N_SEL = 16
NSA_WINDOW = 512
MOBA_BLOCK = 256
MOBA_TOPK = 3
REL_BUCKETS = 32
REL_MAX_DIST = 2048
RMS_EPS = 1e-6
ATTN_SCALE = HEAD_DIM ** -0.5
NEG = -1e30
FORCED = 1e4

V7X_VMEM_LIMIT_BYTES = 56 * 1024 * 1024
LANES = 128
SUBLANES = 8
COL_TILE = 512
ROW_TILE = 1024
IN_ROW_TILE = 512
ATT_TILE = 256
SAMPLE_ROWS = 16
SLABS_PER_TILE = COL_TILE // HEAD_DIM


def _cparams(sem):
    return pltpu.CompilerParams(dimension_semantics=sem, vmem_limit_bytes=V7X_VMEM_LIMIT_BYTES)


def _dot(a, b):
    return jnp.dot(a, b, preferred_element_type=F32)


def _dot_nt(a, b):
    return lax.dot_general(a, b, (((1,), (1,)), ((), ())), preferred_element_type=F32)


def _split_bf16(a, terms):
    parts = []
    for _ in range(terms):
        p = a.astype(BF16)
        parts.append(p)
        a = a - p.astype(F32)
    return parts


def _dot_split(a, b_bf16, terms=2):
    return sum(_dot(p, b_bf16) for p in _split_bf16(a, terms))


def _iota(shape, dim):
    return lax.broadcasted_iota(I32, shape, dim)


def _rms(x, g):
    return (x * lax.rsqrt(jnp.mean(x * x, axis=-1, keepdims=True) + RMS_EPS)) * g


def _softplus(z):
    return jnp.maximum(z, 0.0) + jnp.log(1.0 + jnp.exp(-jnp.abs(z)))


def _head(a, h):
    return a[:, h * HEAD_DIM:(h + 1) * HEAD_DIM]


def _slab(ref, s, n, slabs):
    return ref[pl.ds(s, n, stride=slabs), :]


N_Q_TILES = MIX_WIDTH // COL_TILE
N_KV_TILES = KV_COLS // COL_TILE
KV_GROUPS = (("nsa", NSA_SLABS, 0), ("win", WIN_SLABS, 2), ("sb", SB_SLABS, 3), ("moba", MOBA_SLABS, 5))


def _in_proj_kernel(x_ref, g_ref, w_ref, wg_ref, *refs):
    q_ref, nsa_ref, win_ref, sb_ref, moba_ref, gate_ref, kvb_ref, h_ref = refs[-8:]
    j = pl.program_id(1)
    tm = x_ref.shape[0]

    @pl.when(j == 0)
    def _():
        h = _rms(x_ref[...], g_ref[...]).astype(BF16)
        h_ref[...] = h
        gate_ref[...] = jax.nn.sigmoid(_dot(h, wg_ref[...]))

    acc = _dot(h_ref[...], w_ref[...])

    @pl.when(j < N_Q_TILES)
    def _():
        q_ref[...] = (acc * ATTN_SCALE).astype(BF16)

    @pl.when(j >= N_Q_TILES)
    def _():
        kvb_ref[...] = acc.astype(BF16)

    for (_, slabs, start), ref in zip(KV_GROUPS, (nsa_ref, win_ref, sb_ref, moba_ref)):
        for tt in range(slabs // SLABS_PER_TILE):
            @pl.when(j == N_Q_TILES + start + tt)
            def _(ref=ref, slabs=slabs, tt=tt):
                for s in range(SLABS_PER_TILE):
                    ref[pl.ds(tt * SLABS_PER_TILE + s, tm, stride=slabs), :] = _head(acc, s)


def _in_proj(x, g, w_main, w_gate, layer, depth, tm, prev_kv):
    m, d = x.shape
    n_tiles = N_Q_TILES + N_KV_TILES
    row_tiles = m // tm

    def clamp_map(lo, count):
        return lambda i, j: (i, jnp.clip(j - lo, 0, count - 1))

    out_shape = [jax.ShapeDtypeStruct((m, MIX_WIDTH), BF16)]
    out_specs = [pl.BlockSpec((tm, COL_TILE), clamp_map(0, N_Q_TILES))]
    for _, slabs, _ in KV_GROUPS:
        out_shape.append(jax.ShapeDtypeStruct((depth * m * slabs, HEAD_DIM), F32))
        out_specs.append(pl.BlockSpec((tm * slabs, HEAD_DIM), lambda i, j: (layer * row_tiles + i, 0)))
    out_shape += [jax.ShapeDtypeStruct((m, 2 * LANES), F32), jax.ShapeDtypeStruct((m, KV_COLS), BF16)]
    out_specs += [pl.BlockSpec((tm, 2 * LANES), lambda i, j: (i, 0)),
                  pl.BlockSpec((tm, COL_TILE), clamp_map(N_Q_TILES, N_KV_TILES))]
    n_in = 4
    return pl.pallas_call(
        _in_proj_kernel,
        grid=(row_tiles, n_tiles),
        in_specs=[
            pl.BlockSpec((tm, d), lambda i, j: (i, 0)),
            pl.BlockSpec((None, 1, d), lambda i, j: (layer, 0, 0)),
            pl.BlockSpec((None, d, COL_TILE), lambda i, j: (layer, 0, j)),
            pl.BlockSpec((None, d, 2 * LANES), lambda i, j: (layer, 0, 0)),
        ] + [pl.BlockSpec(memory_space=pl.ANY)] * len(prev_kv),
        out_specs=out_specs,
        out_shape=out_shape,
        input_output_aliases={n_in + k: 1 + k for k in range(len(prev_kv))},
        scratch_shapes=[pltpu.VMEM((tm, d), BF16)],
        compiler_params=_cparams(("arbitrary", "arbitrary")),
        name="in_proj",
    )(x, g, w_main, w_gate, *prev_kv)


def _out_proj_kernel(x_ref, a_ref, b_ref, c_ref, w_ref, o_ref):
    n_a = a_ref.shape[1]
    n_b = b_ref.shape[1]
    acc = _dot(a_ref[...], w_ref[0:n_a, :])
    acc += _dot(b_ref[...], w_ref[n_a:n_a + n_b, :])
    acc += _dot(c_ref[...], w_ref[n_a + n_b:, :])
    o_ref[...] = x_ref[...] + acc


def _out_proj(x, mix_nsa, mix_sb, mix_moba, w_out, layer, tm):
    m, d = x.shape
    return pl.pallas_call(
        _out_proj_kernel,
        grid=(m // tm, d // COL_TILE),
        in_specs=[
            pl.BlockSpec((tm, COL_TILE), lambda i, j: (i, j)),
            pl.BlockSpec((tm, mix_nsa.shape[1]), lambda i, j: (i, 0)),
            pl.BlockSpec((tm, mix_sb.shape[1]), lambda i, j: (i, 0)),
            pl.BlockSpec((tm, mix_moba.shape[1]), lambda i, j: (i, 0)),
            pl.BlockSpec((None, MIX_WIDTH, COL_TILE), lambda i, j: (layer, 0, j)),
        ],
        out_specs=pl.BlockSpec((tm, COL_TILE), lambda i, j: (i, j)),
        out_shape=jax.ShapeDtypeStruct((m, d), F32),
        compiler_params=_cparams(("arbitrary", "arbitrary")),
        name="out_proj",
    )(x, mix_nsa, mix_sb, mix_moba, w_out)


def _ffn_kernel(x_ref, g_ref, wu_ref, wd_ref, gf_ref, o_ref, h_ref, *, final_norm):
    j = pl.program_id(1)

    @pl.when(j == 0)
    def _():
        x = x_ref[...]
        h_ref[...] = _rms(x, g_ref[...]).astype(BF16)
        o_ref[...] = x

    a = jnp.maximum(_dot(h_ref[...], wu_ref[...]), 0.0)
    o_ref[...] += _dot((a * a).astype(BF16), wd_ref[...])

    if final_norm:
        @pl.when(j == pl.num_programs(1) - 1)
        def _():
            o_ref[...] = _rms(o_ref[...], gf_ref[...])


def _ffn(x, g, w_up, w_down, g_final, layer, tm, final_norm):
    m, d = x.shape
    d_ff = w_up.shape[2]
    return pl.pallas_call(
        functools.partial(_ffn_kernel, final_norm=final_norm),
        grid=(m // tm, d_ff // COL_TILE),
        in_specs=[
            pl.BlockSpec((tm, d), lambda i, j: (i, 0)),
            pl.BlockSpec((None, 1, d), lambda i, j: (layer, 0, 0)),
            pl.BlockSpec((None, d, COL_TILE), lambda i, j: (layer, 0, j)),
            pl.BlockSpec((None, COL_TILE, d), lambda i, j: (layer, j, 0)),
            pl.BlockSpec((1, d), lambda i, j: (0, 0)),
        ],
        out_specs=pl.BlockSpec((tm, d), lambda i, j: (i, 0)),
        out_shape=jax.ShapeDtypeStruct((m, d), F32),
        scratch_shapes=[pltpu.VMEM((tm, d), BF16)],
        compiler_params=_cparams(("arbitrary", "arbitrary")),
        name="ffn",
    )(x, g, w_up, w_down, g_final)


def _bias_tiles_kernel(thr_ref, tbl_ref, o_ref):
    h = pl.program_id(0)
    d = pl.program_id(1)
    t = ATT_TILE
    dist = d * t + _iota((t, t), 0) - _iota((t, t), 1)
    acc = jnp.full((t, t), tbl_ref[h, 0], F32)
    for k in range(1, REL_BUCKETS):
        acc = jnp.where(dist >= thr_ref[k], tbl_ref[h, k], acc)
    o_ref[...] = acc


def _bias_tiles(thr, tbl_t, nd):
    n_heads = tbl_t.shape[0]
    grid_spec = pltpu.PrefetchScalarGridSpec(
        num_scalar_prefetch=2,
        grid=(n_heads, nd),
        in_specs=[],
        out_specs=pl.BlockSpec((None, None, ATT_TILE, ATT_TILE), lambda h, d, thr, tbl: (h, d, 0, 0)),
    )
    return pl.pallas_call(
        _bias_tiles_kernel,
        grid_spec=grid_spec,
        out_shape=jax.ShapeDtypeStruct((n_heads, nd, ATT_TILE, ATT_TILE), F32),
        compiler_params=_cparams(("arbitrary", "arbitrary")),
        name="bias_tiles",
    )(thr, tbl_t)


def _online_step(m_s, l_s, acc_s, slot, s, mask, v):
    if mask is not None:
        s = jnp.where(mask, s, NEG)
    m_prev = m_s[slot]
    m_new = jnp.maximum(m_prev, jnp.max(s, axis=-1, keepdims=True))
    alpha = jnp.exp(m_prev - m_new)
    p = jnp.exp(s - m_new)
    l_s[slot] = alpha * l_s[slot] + jnp.sum(p, axis=-1, keepdims=True)
    acc_s[slot] = alpha * acc_s[slot] + _dot(p.astype(BF16), v)
    m_s[slot] = m_new


def _reset_state(m_s, l_s, acc_s):
    m_s[...] = jnp.full(m_s.shape, NEG, F32)
    l_s[...] = jnp.zeros(l_s.shape, F32)
    acc_s[...] = jnp.zeros(acc_s.shape, F32)


def _tile(ref, j, t):
    return ref[pl.ds(pl.multiple_of(j * t, t), t), :]


def _rank_desc(score, n):
    idx = _iota((1, n), 1)
    rank = jnp.zeros(score.shape, F32)
    for mm in range(n):
        sm = score[:, mm:mm + 1]
        first = (mm < idx).astype(F32)
        rank += jnp.where(sm > score, 1.0, jnp.where(sm == score, first, 0.0))
    return rank


def _sb_prompt_kernel(q_ref, k_ref, v_ref, o_ref, carry_s, acc_s):
    t = ATT_TILE
    i = pl.program_id(1)
    r = _iota((t, t), 0)
    c = _iota((t, t), 1)
    u_incl = (r >= c).astype(BF16)
    strict = r > c
    carry_s[...] = jnp.zeros(carry_s.shape, F32)
    acc_s[...] = jnp.zeros(acc_s.shape, F32)

    def step(j, mask):
        kt = _tile(k_ref, j, t)
        vt = _tile(v_ref, j, t)
        for h in range(SB_HEADS):
            z = _dot_nt(_head(q_ref, h), _head(kt, h))
            sp = _softplus(z)
            lk = -sp if mask is None else jnp.where(mask, -sp, 0.0)
            incl = _dot_split(lk, u_incl)
            carry = carry_s[h]
            a = jnp.exp(z - sp + (carry + incl - lk))
            if mask is not None:
                a = jnp.where(mask, a, 0.0)
            acc_s[h] += _dot(a.astype(BF16), _head(vt, h))
            carry_s[h] = carry + incl[:, 0:1]

    step(i, strict)

    def body(d, _):
        step(i - d, None)
        return 0

    lax.fori_loop(1, i + 1, body, 0)
    for h in range(SB_HEADS):
        o_ref[:, h * HEAD_DIM:(h + 1) * HEAD_DIM] = acc_s[h].astype(o_ref.dtype)


def _sb_prompt(q, kvb, batch, seq):
    nq = seq // ATT_TILE
    hw = SB_HEADS * HEAD_DIM
    kv0 = (NSA_ROW_COLS + WIN_ROW_COLS) // hw
    q0 = NSA_HEADS * HEAD_DIM // hw
    return pl.pallas_call(
        _sb_prompt_kernel,
        grid=(batch, nq),
        in_specs=[
            pl.BlockSpec((ATT_TILE, hw), lambda b, i: (b * nq + i, q0)),
            pl.BlockSpec((seq, hw), lambda b, i: (b, kv0)),
            pl.BlockSpec((seq, hw), lambda b, i: (b, kv0 + 1)),
        ],
        out_specs=pl.BlockSpec((ATT_TILE, hw), lambda b, i: (b * nq + i, 0)),
        out_shape=jax.ShapeDtypeStruct((batch * seq, hw), BF16),
        scratch_shapes=[pltpu.VMEM((SB_HEADS, ATT_TILE, 1), F32),
                        pltpu.VMEM((SB_HEADS, ATT_TILE, HEAD_DIM), F32)],
        compiler_params=_cparams(("arbitrary", "arbitrary")),
        name="sb_prompt",
    )(q, kvb, kvb)


def _moba_prompt_kernel(q_ref, k_ref, v_ref, bias_ref, o_ref, kmean_s, sel_s, m_s, l_s, acc_s):
    t = ATT_TILE
    i = pl.program_id(1)
    seq = k_ref.shape[0]
    nb = seq // t

    @pl.when(i == 0)
    def _():
        avg = jnp.where(_iota((nb, seq), 1) // t == _iota((nb, seq), 0), 1.0 / t, 0.0).astype(BF16)
        kmean_s[...] = _dot(avg, k_ref[...])

    ncol = _iota((1, nb), 1)
    past = ncol < i
    for h in range(MOBA_HEADS):
        q = _head(q_ref, h)
        km_hi, km_lo = _split_bf16(_head(kmean_s, h), 2)
        gs = _dot_nt(q, km_hi) + _dot_nt(q, km_lo)
        rank = _rank_desc(jnp.where(past, gs, NEG), nb)
        sel_s[h] = jnp.where(past & (rank < MOBA_TOPK), 1.0, 0.0)

    _reset_state(m_s, l_s, acc_s)
    causal = _iota((t, t), 0) >= _iota((t, t), 1)

    def step(d, diagonal):
        j = i - d
        kt = _tile(k_ref, j, t)
        vt = _tile(v_ref, j, t)
        for h in range(MOBA_HEADS):
            if diagonal:
                mask = causal
            else:
                mask = jnp.sum(jnp.where(ncol == j, sel_s[h], 0.0), axis=-1, keepdims=True) > 0.5
            s = _dot_nt(_head(q_ref, h), _head(kt, h)) + bias_ref[h, d]
            _online_step(m_s, l_s, acc_s, h, s, mask, _head(vt, h))

    step(0, True)

    def body(d, _):
        step(d, False)
        return 0

    lax.fori_loop(1, i + 1, body, 0)
    for h in range(MOBA_HEADS):
        o_ref[:, h * HEAD_DIM:(h + 1) * HEAD_DIM] = (acc_s[h] / l_s[h]).astype(o_ref.dtype)


def _moba_prompt(q, kvb, bias_tiles, batch, seq):
    nq = seq // ATT_TILE
    hw = MOBA_HEADS * HEAD_DIM
    kv0 = (NSA_ROW_COLS + WIN_ROW_COLS + SB_ROW_COLS) // hw
    q0 = (NSA_HEADS + SB_HEADS) * HEAD_DIM // hw
    nd = bias_tiles.shape[1]
    return pl.pallas_call(
        _moba_prompt_kernel,
        grid=(batch, nq),
        in_specs=[
            pl.BlockSpec((ATT_TILE, hw), lambda b, i: (b * nq + i, q0)),
            pl.BlockSpec((seq, hw), lambda b, i: (b, kv0)),
            pl.BlockSpec((seq, hw), lambda b, i: (b, kv0 + 1)),
            pl.BlockSpec((MOBA_HEADS, nd, ATT_TILE, ATT_TILE), lambda b, i: (NSA_HEADS // MOBA_HEADS, 0, 0, 0)),
        ],
        out_specs=pl.BlockSpec((ATT_TILE, hw), lambda b, i: (b * nq + i, 0)),
        out_shape=jax.ShapeDtypeStruct((batch * seq, hw), BF16),
        scratch_shapes=[
            pltpu.VMEM((seq // MOBA_BLOCK, hw), F32),
            pltpu.VMEM((MOBA_HEADS, ATT_TILE, seq // MOBA_BLOCK), F32),
            pltpu.VMEM((MOBA_HEADS, ATT_TILE, 1), F32),
            pltpu.VMEM((MOBA_HEADS, ATT_TILE, 1), F32),
            pltpu.VMEM((MOBA_HEADS, ATT_TILE, HEAD_DIM), F32),
        ],
        compiler_params=_cparams(("arbitrary", "arbitrary")),
        name="moba_prompt",
    )(q, kvb, kvb, bias_tiles)


def _compress_mlp(z, w1, w2):
    hid = _dot(z.astype(BF16), w1.astype(BF16))
    hid = hid * jax.nn.sigmoid(hid)
    return _dot(hid.astype(BF16), w2.astype(BF16))


def _nsa_prompt_kernel(q_ref, ck_ref, cv_ref, sk_ref, sv_ref, wk_ref, wv_ref, alpha_ref, pe_ref,
                       w1_ref, w2_ref, biasc_ref, bias_ref, gate_ref, o_ref,
                       kc_s, vc_s, ocmp_s, m_s, l_s, acc_s):
    t = ATT_TILE
    i = pl.program_id(2)
    seq = ck_ref.shape[0]
    nb = seq // NSA_BLOCK
    per_tile = t // NSA_BLOCK

    @pl.when(i == 0)
    def _():
        for c, (src, dst) in enumerate(((ck_ref, kc_s), (cv_ref, vc_s))):
            x = src[...].astype(F32).reshape(nb, NSA_BLOCK, HEAD_DIM)
            z = jnp.sum((x + pe_ref[c][None]) * alpha_ref[c][None], axis=1)
            dst[...] = _compress_mlp(z, w1_ref[c], w2_ref[c]).astype(BF16)

    tpos = i * t + _iota((t, 1), 0)
    ncol = _iota((1, nb), 1)
    complete = tpos >= ncol * NSA_BLOCK + (NSA_BLOCK - 1)
    kc = kc_s[...]
    vc = vc_s[...]
    imp = jnp.zeros((t, nb), F32)
    for jh in range(NSA_GROUP):
        z = jnp.where(complete, _dot_nt(_head(q_ref, jh), kc) + biasc_ref[jh], NEG)
        z = z - jnp.max(z, axis=-1, keepdims=True)
        e = jnp.where(complete, jnp.exp(z), 0.0)
        p = e / jnp.maximum(jnp.sum(e, axis=-1, keepdims=True), 1e-30)
        imp += p
        ocmp_s[jh] = _dot(p.astype(BF16), vc)

    forced = (ncol == tpos // NSA_BLOCK) | (ncol == 0)
    score = jnp.where(forced, FORCED, jnp.where(complete, imp, NEG))
    rank = _rank_desc(score, nb)
    sel = jnp.where((rank < NSA_N_SEL) & (score > 0.5 * NEG), 1.0, 0.0).astype(BF16)

    def key_mask(j):
        expand = (_iota((nb, t), 0) == j * per_tile + _iota((nb, t), 1) // NSA_BLOCK).astype(BF16)
        return _dot(sel, expand) > 0.5

    _reset_state(m_s, l_s, acc_s)
    rr = _iota((t, t), 0)
    cc = _iota((t, t), 1)

    def tile_step(d, with_window):
        j = i - d
        sk = _tile(sk_ref, j, t)
        sv = _tile(sv_ref, j, t)
        slc_mask = key_mask(j)
        if with_window:
            wk = _tile(wk_ref, j, t)
            wv = _tile(wv_ref, j, t)
            dist = d * t + rr - cc
            win_mask = (dist >= 0) & (dist < NSA_WINDOW)
            slc_mask = slc_mask & (dist >= 0)
        for jh in range(NSA_GROUP):
            qh = _head(q_ref, jh)
            bias = bias_ref[jh, d]
            _online_step(m_s, l_s, acc_s, jh, _dot_nt(qh, sk) + bias, slc_mask, sv)
            if with_window:
                _online_step(m_s, l_s, acc_s, NSA_GROUP + jh, _dot_nt(qh, wk) + bias, win_mask, wv)

    tile_step(0, True)
    n_win = NSA_WINDOW // t

    def body_win(d, _):
        tile_step(d, True)
        return 0

    def body_far(d, _):
        tile_step(d, False)
        return 0

    lax.fori_loop(1, jnp.minimum(i, n_win) + 1, body_win, 0)
    lax.fori_loop(n_win + 1, i + 1, body_far, 0)

    gates = gate_ref[...]
    for jh in range(NSA_GROUP):
        g_cmp = gates[:, jh:jh + 1]
        g_slc = gates[:, NSA_GROUP + jh:NSA_GROUP + jh + 1]
        g_win = gates[:, 2 * NSA_GROUP + jh:2 * NSA_GROUP + jh + 1]
        o = (g_cmp * ocmp_s[jh] + g_slc * (acc_s[jh] / l_s[jh])
             + g_win * (acc_s[NSA_GROUP + jh] / l_s[NSA_GROUP + jh]))
        o_ref[:, jh * HEAD_DIM:(jh + 1) * HEAD_DIM] = o.astype(o_ref.dtype)


def _nsa_prompt(q, kvb, gates, alpha_b, pe, w1, w2, biasc, bias_tiles, layer, batch, seq):
    nq = seq // ATT_TILE
    nb = seq // NSA_BLOCK
    nd = bias_tiles.shape[1]
    gw = NSA_GROUP * HEAD_DIM
    win0 = NSA_ROW_COLS // HEAD_DIM

    def col(c):
        return pl.BlockSpec((seq, HEAD_DIM), lambda b, g, i: (b, c + g))

    return pl.pallas_call(
        _nsa_prompt_kernel,
        grid=(batch, NSA_KV, nq),
        in_specs=[
            pl.BlockSpec((ATT_TILE, gw), lambda b, g, i: (b * nq + i, g)),
            col(0), col(NSA_KV), col(2 * NSA_KV), col(3 * NSA_KV),
            col(win0), col(win0 + NSA_KV),
            pl.BlockSpec((None, 2, NSA_BLOCK, HEAD_DIM), lambda b, g, i: (layer, 0, 0, 0)),
            pl.BlockSpec((None, 2, NSA_BLOCK, HEAD_DIM), lambda b, g, i: (layer, 0, 0, 0)),
            pl.BlockSpec((None, 2, HEAD_DIM, HEAD_DIM), lambda b, g, i: (layer, 0, 0, 0)),
            pl.BlockSpec((None, 2, HEAD_DIM, HEAD_DIM), lambda b, g, i: (layer, 0, 0, 0)),
            pl.BlockSpec((None, NSA_GROUP, ATT_TILE, nb), lambda b, g, i: (g, 0, i, 0)),
            pl.BlockSpec((NSA_GROUP, nd, ATT_TILE, ATT_TILE), lambda b, g, i: (g, 0, 0, 0)),
            pl.BlockSpec((ATT_TILE, LANES), lambda b, g, i: (b * nq + i, g)),
        ],
        out_specs=pl.BlockSpec((ATT_TILE, gw), lambda b, g, i: (b * nq + i, g)),
        out_shape=jax.ShapeDtypeStruct((batch * seq, NSA_HEADS * HEAD_DIM), BF16),
        scratch_shapes=[
            pltpu.VMEM((nb, HEAD_DIM), BF16),
            pltpu.VMEM((nb, HEAD_DIM), BF16),
            pltpu.VMEM((NSA_GROUP, ATT_TILE, HEAD_DIM), F32),
            pltpu.VMEM((2 * NSA_GROUP, ATT_TILE, 1), F32),
            pltpu.VMEM((2 * NSA_GROUP, ATT_TILE, 1), F32),
            pltpu.VMEM((2 * NSA_GROUP, ATT_TILE, HEAD_DIM), F32),
        ],
        compiler_params=_cparams(("arbitrary", "arbitrary", "arbitrary")),
        name="nsa_prompt",
    )(q, kvb, kvb, kvb, kvb, kvb, kvb, alpha_b, pe, w1, w2, biasc, bias_tiles, gates)


SB_PAGES_PER_STEP = 8
MOBA_PAGES_PER_STEP = 16
NSA_PAGES_PER_STEP = 4


def _sb_sample_kernel(pt_ref, q_ref, *refs):
    del pt_ref
    pages = refs[:SB_PAGES_PER_STEP]
    o_ref, carry_s, acc_s = refs[SB_PAGES_PER_STEP:]
    s = pl.program_id(1)
    hw = SB_HEADS * HEAD_DIM
    page = pages[0].shape[0] // SB_SLABS

    @pl.when(s == 0)
    def _():
        carry_s[...] = jnp.zeros(carry_s.shape, F32)
        acc_s[...] = jnp.zeros(acc_s.shape, F32)

    own = _iota((SUBLANES, hw), 1) // HEAD_DIM == _iota((SUBLANES, hw), 0)
    q_bd = jnp.where(own, q_ref[...], 0.0).astype(BF16)
    u_incl = (_iota((page, page), 0) >= _iota((page, page), 1)).astype(BF16)
    carry = carry_s[...]
    acc = acc_s[...]
    for ref in pages:
        k = jnp.concatenate([_slab(ref, h, page, SB_SLABS) for h in range(SB_HEADS)], axis=1).astype(BF16)
        v = jnp.concatenate([_slab(ref, SB_HEADS + h, page, SB_SLABS) for h in range(SB_HEADS)],
                            axis=1).astype(BF16)
        z = _dot_nt(q_bd, k)
        sp = _softplus(z)
        lk = -sp
        incl = _dot_split(lk, u_incl)
        a = jnp.exp(z - sp + (carry + incl - lk))
        acc = acc + _dot(a.astype(BF16), v)
        carry = carry + incl[:, 0:1]
    carry_s[...] = carry
    acc_s[...] = acc

    @pl.when(s == pl.num_programs(1) - 1)
    def _():
        o_ref[...] = jnp.sum(jnp.where(own, acc, 0.0), axis=0, keepdims=True)


def _sb_sample(page_table, q3, cache_rows, layer):
    batch, n_pages = page_table.shape
    hw = SB_HEADS * HEAD_DIM
    steps = n_pages // SB_PAGES_PER_STEP

    def page_map(k):
        return lambda b, s, pt: (layer, pt[b, n_pages - 1 - (s * SB_PAGES_PER_STEP + k)], 0, 0)

    grid_spec = pltpu.PrefetchScalarGridSpec(
        num_scalar_prefetch=1,
        grid=(batch, steps),
        in_specs=[pl.BlockSpec((None, 1, hw), lambda b, s, pt: (b, 0, NSA_HEADS * HEAD_DIM // hw))]
        + [pl.BlockSpec((None, None) + cache_rows.shape[2:], page_map(k)) for k in range(SB_PAGES_PER_STEP)],
        out_specs=pl.BlockSpec((None, 1, hw), lambda b, s, pt: (b, 0, 0)),
        scratch_shapes=[pltpu.VMEM((SUBLANES, 1), F32), pltpu.VMEM((SUBLANES, hw), F32)],
    )
    return pl.pallas_call(
        _sb_sample_kernel,
        grid_spec=grid_spec,
        out_shape=jax.ShapeDtypeStruct((batch, 1, hw), F32),
        compiler_params=_cparams(("arbitrary", "arbitrary")),
        name="sb_sample",
    )(page_table, q3, *([cache_rows] * SB_PAGES_PER_STEP))


def _moba_select_kernel(pt_ref, q_ref, *refs):
    del pt_ref
    pages = refs[:MOBA_PAGES_PER_STEP]
    idx_ref, kmean_s = refs[MOBA_PAGES_PER_STEP:]
    s = pl.program_id(1)
    page = pages[0].shape[0] // MOBA_SLABS
    per_block = MOBA_BLOCK // page
    blocks = MOBA_PAGES_PER_STEP // per_block
    rows = blocks * MOBA_SLABS
    nb = kmean_s.shape[0] // MOBA_SLABS

    sums = [jnp.sum(ref[...].reshape(page, MOBA_SLABS, HEAD_DIM), axis=0) for ref in pages]
    means = [sum(sums[r * per_block:(r + 1) * per_block]) * (1.0 / MOBA_BLOCK) for r in range(blocks)]
    kmean_s[pl.ds(pl.multiple_of(s * rows, rows), rows), :] = jnp.concatenate(means, axis=0)

    @pl.when(s == pl.num_programs(1) - 1)
    def _():
        prod = (kmean_s[...].reshape(nb, MOBA_SLABS, HEAD_DIM) * q_ref[...][None]).reshape(nb * MOBA_SLABS, HEAD_DIM)
        ones = jnp.ones((HEAD_DIM, LANES), BF16)
        gs = _dot_split(prod, ones, 3).reshape(nb, MOBA_SLABS, LANES)
        n_idx = _iota((nb, MOBA_SLABS, LANES), 0)
        rank = jnp.zeros((nb, MOBA_SLABS, LANES), F32)
        for mm in range(nb):
            gm = gs[mm][None]
            first = (mm < n_idx).astype(F32)
            rank += jnp.where(gm > gs, 1.0, jnp.where(gm == gs, first, 0.0))
        n_f = n_idx.astype(F32)
        for k in range(MOBA_TOPK):
            idx_ref[k] = jnp.sum(jnp.where(rank == float(k), n_f, 0.0), axis=0).astype(I32)


def _moba_select(page_table, q8, cache_rows, layer):
    batch, n_pages = page_table.shape
    page = cache_rows.shape[2] // MOBA_SLABS
    steps = n_pages // MOBA_PAGES_PER_STEP
    nb = n_pages * page // MOBA_BLOCK

    def page_map(k):
        return lambda b, s, pt: (layer, pt[b, s * MOBA_PAGES_PER_STEP + k], 0, 0)

    grid_spec = pltpu.PrefetchScalarGridSpec(
        num_scalar_prefetch=1,
        grid=(batch, steps),
        in_specs=[pl.BlockSpec((None, MOBA_SLABS, HEAD_DIM), lambda b, s, pt: (b, 0, 0))]
        + [pl.BlockSpec((None, None) + cache_rows.shape[2:], page_map(k)) for k in range(MOBA_PAGES_PER_STEP)],
        out_specs=pl.BlockSpec((None, MOBA_TOPK, MOBA_SLABS, LANES), lambda b, s, pt: (b, 0, 0, 0)),
        scratch_shapes=[pltpu.VMEM((nb * MOBA_SLABS, HEAD_DIM), F32)],
    )
    return pl.pallas_call(
        _moba_select_kernel,
        grid_spec=grid_spec,
        out_shape=jax.ShapeDtypeStruct((batch, MOBA_TOPK, MOBA_SLABS, LANES), I32),
        compiler_params=_cparams(("arbitrary", "arbitrary")),
        name="moba_select",
    )(page_table, q8, *([cache_rows] * MOBA_PAGES_PER_STEP))


def _moba_sample_kernel(pt_ref, idx_ref, q_ref, knew_ref, vnew_ref, bias0_ref, *refs):
    del pt_ref, idx_ref
    n = (len(refs) - 1) // 2
    page_refs, b_refs = refs[0:n], refs[n:2 * n]
    o_ref = refs[2 * n]
    h = pl.program_id(1)
    q = q_ref[...]
    q8 = jnp.broadcast_to(q, (SUBLANES, HEAD_DIM)).astype(BF16)
    m = jnp.sum(q * knew_ref[...], axis=-1, keepdims=True) + bias0_ref[:, 0:1]
    l = jnp.ones((1, 1), F32)
    acc = vnew_ref[...]
    for page_ref, b_ref in zip(page_refs, b_refs):
        page = page_ref.shape[0] // MOBA_SLABS
        k = _slab(page_ref, h, page, MOBA_SLABS).astype(BF16)
        v = _slab(page_ref, MOBA_HEADS + h, page, MOBA_SLABS).astype(BF16)
        s = (_dot_nt(q8, k) + b_ref[...])[0:1, :]
        m_new = jnp.maximum(m, jnp.max(s, axis=-1, keepdims=True))
        alpha = jnp.exp(m - m_new)
        p = jnp.exp(s - m_new)
        l = alpha * l + jnp.sum(p, axis=-1, keepdims=True)
        p8 = jnp.broadcast_to(p, (SUBLANES, p.shape[1])).astype(BF16)
        acc = alpha * acc + _dot(p8, v)[0:1, :]
        m = m_new
    o_ref[...] = acc / l


def _moba_sample(page_table, idx, q3, cache_rows, rows4, bias_pages, bias0, layer):
    batch, n_pages = page_table.shape
    page = cache_rows.shape[2] // MOBA_SLABS
    per_block = MOBA_BLOCK // page
    n = MOBA_TOPK * per_block

    def pg(b, h, u, ix):
        return ix[b, u // per_block, h] * per_block + u % per_block

    def page_spec(u):
        return pl.BlockSpec((None, None) + cache_rows.shape[2:],
                            lambda b, h, pt, ix: (layer, pt[b, pg(b, h, u, ix)], 0, 0))

    def bias_spec(u):
        return pl.BlockSpec((None, None, 1, page), lambda b, h, pt, ix: (h, pg(b, h, u, ix), 0, 0))

    grid_spec = pltpu.PrefetchScalarGridSpec(
        num_scalar_prefetch=2,
        grid=(batch, MOBA_HEADS),
        in_specs=[
            pl.BlockSpec((None, 1, HEAD_DIM), lambda b, h, pt, ix: (b, 0, NSA_HEADS + SB_HEADS + h)),
            pl.BlockSpec((None, None, 1, HEAD_DIM), lambda b, h, pt, ix: (b, h, 0, 0)),
            pl.BlockSpec((None, None, 1, HEAD_DIM), lambda b, h, pt, ix: (b, MOBA_HEADS + h, 0, 0)),
            pl.BlockSpec((None, 1, LANES), lambda b, h, pt, ix: (h, 0, 0)),
        ]
        + [page_spec(u) for u in range(n)] + [bias_spec(u) for u in range(n)],
        out_specs=pl.BlockSpec((None, 1, HEAD_DIM), lambda b, h, pt, ix: (b, 0, h)),
    )
    return pl.pallas_call(
        _moba_sample_kernel,
        grid_spec=grid_spec,
        out_shape=jax.ShapeDtypeStruct((batch, 1, MOBA_HEADS * HEAD_DIM), F32),
        compiler_params=_cparams(("arbitrary", "arbitrary")),
        name="moba_sample",
    )(page_table, idx, q3, rows4, rows4, bias0, *([cache_rows] * n), *([bias_pages] * n))


def _nsa_select_kernel(pt_ref, qg_ref, alpha_ref, pe_ref, w1_ref, w2_ref, biasc_ref, *refs):
    del pt_ref
    pages = refs[:NSA_PAGES_PER_STEP]
    ocmp_ref, idx_ref, z_s = refs[NSA_PAGES_PER_STEP:]
    s = pl.program_id(1)
    page = pages[0].shape[0] // NSA_SLABS
    per_page = page // NSA_BLOCK
    rows = NSA_PAGES_PER_STEP * per_page * NSA_SLABS
    nb = z_s.shape[0] // NSA_SLABS

    alpha = alpha_ref[...]
    pe = pe_ref[...]
    zs = [jnp.sum((ref[...].reshape(per_page, NSA_BLOCK, NSA_SLABS, HEAD_DIM) + pe[None]) * alpha[None], axis=1)
          .reshape(per_page * NSA_SLABS, HEAD_DIM) for ref in pages]
    z_s[pl.ds(pl.multiple_of(s * rows, rows), rows), :] = jnp.concatenate(zs, axis=0)

    @pl.when(s == pl.num_programs(1) - 1)
    def _():
        n_lane = _iota((1, nb), 1)
        for g in range(NSA_KV):
            kc, vc = [_compress_mlp(_slab(z_s, c * NSA_KV + g, nb, NSA_SLABS), w1_ref[c], w2_ref[c]).astype(BF16)
                      for c in range(2)]
            lc = _dot_nt(qg_ref[g].astype(BF16), kc) + biasc_ref[g]
            lc = lc - jnp.max(lc, axis=-1, keepdims=True)
            e = jnp.exp(lc)
            p = e / jnp.maximum(jnp.sum(e, axis=-1, keepdims=True), 1e-30)
            ocmp_ref[g] = _dot(p.astype(BF16), vc)
            imp = jnp.sum(p[0:NSA_GROUP], axis=0, keepdims=True)
            by_lane = jnp.broadcast_to(imp, (nb, nb))
            by_row = by_lane.T
            m_idx = _iota((nb, nb), 0)
            n_idx = _iota((nb, nb), 1)
            beats = jnp.where(by_row > by_lane, 1.0,
                              jnp.where((by_row == by_lane) & (m_idx < n_idx), 1.0, 0.0))
            beats = jnp.where(m_idx >= 1, beats, 0.0)
            rank = jnp.sum(beats, axis=0, keepdims=True)
            k_idx = _iota((NSA_N_SEL, nb), 0)
            hit = ((jnp.broadcast_to(rank, (NSA_N_SEL, nb)) == k_idx.astype(F32)) & (n_lane >= 1)
                   & (k_idx < NSA_N_SEL - 2))
            pick = jnp.sum(jnp.where(hit, n_lane.astype(F32), 0.0), axis=-1, keepdims=True)
            idx_ref[g] = jnp.broadcast_to(pick, (NSA_N_SEL, LANES)).astype(I32)


def _nsa_select(page_table, qg, cache_rows, alpha_w, pe_w, w1, w2, biasc_s, layer):
    batch, n_pages = page_table.shape
    page = cache_rows.shape[2] // NSA_SLABS
    steps = n_pages // NSA_PAGES_PER_STEP
    nb = n_pages * page // NSA_BLOCK

    def page_map(k):
        return lambda b, s, pt: (layer, pt[b, s * NSA_PAGES_PER_STEP + k], 0, 0)

    grid_spec = pltpu.PrefetchScalarGridSpec(
        num_scalar_prefetch=1,
        grid=(batch, steps),
        in_specs=[
            pl.BlockSpec((None, NSA_KV, SUBLANES, HEAD_DIM), lambda b, s, pt: (b, 0, 0, 0)),
            pl.BlockSpec((None, NSA_BLOCK, NSA_SLABS, HEAD_DIM), lambda b, s, pt: (layer, 0, 0, 0)),
            pl.BlockSpec((None, NSA_BLOCK, NSA_SLABS, HEAD_DIM), lambda b, s, pt: (layer, 0, 0, 0)),
            pl.BlockSpec((None, 2, HEAD_DIM, HEAD_DIM), lambda b, s, pt: (layer, 0, 0, 0)),
            pl.BlockSpec((None, 2, HEAD_DIM, HEAD_DIM), lambda b, s, pt: (layer, 0, 0, 0)),
            pl.BlockSpec((NSA_KV, SUBLANES, nb), lambda b, s, pt: (0, 0, 0)),
        ] + [pl.BlockSpec((None, None) + cache_rows.shape[2:], page_map(k)) for k in range(NSA_PAGES_PER_STEP)],
        out_specs=[
            pl.BlockSpec((None, NSA_KV, SUBLANES, HEAD_DIM), lambda b, s, pt: (b, 0, 0, 0)),
            pl.BlockSpec((None, NSA_KV, NSA_N_SEL, LANES), lambda b, s, pt: (b, 0, 0, 0)),
        ],
        scratch_shapes=[pltpu.VMEM((nb * NSA_SLABS, HEAD_DIM), F32)],
    )
    return pl.pallas_call(
        _nsa_select_kernel,
        grid_spec=grid_spec,
        out_shape=[jax.ShapeDtypeStruct((batch, NSA_KV, SUBLANES, HEAD_DIM), F32),
                   jax.ShapeDtypeStruct((batch, NSA_KV, NSA_N_SEL, LANES), I32)],
        compiler_params=_cparams(("arbitrary", "arbitrary")),
        name="nsa_select",
    )(page_table, qg, alpha_w, pe_w, w1, w2, biasc_s, *([cache_rows] * NSA_PAGES_PER_STEP))


def _nsa_sample_kernel(pt_ref, idx_ref, qg_ref, ocmp_ref, gate_ref, sknew_ref, svnew_ref, wknew_ref, wvnew_ref,
                       win_ref, biasw_ref, bias0_ref, biasb_ref, *refs):
    del pt_ref
    blk_refs, o_ref = refs[:-1], refs[-1]
    b = pl.program_id(0)
    g = pl.program_id(1)
    qf = qg_ref[...]
    q = qf.astype(BF16)
    bias0 = bias0_ref[:, 0:1]

    m = jnp.sum(qf * sknew_ref[...], axis=-1, keepdims=True) + bias0
    l = jnp.ones((SUBLANES, 1), F32)
    acc = jnp.broadcast_to(svnew_ref[...], (SUBLANES, HEAD_DIM))
    for u, ref in enumerate(blk_refs):
        k = _slab(ref, 2 * NSA_KV + g, NSA_BLOCK, NSA_SLABS).astype(BF16)
        v = _slab(ref, 3 * NSA_KV + g, NSA_BLOCK, NSA_SLABS).astype(BF16)
        s = _dot_nt(q, k) + biasb_ref[idx_ref[b, g, u]]
        m_new = jnp.maximum(m, jnp.max(s, axis=-1, keepdims=True))
        alpha = jnp.exp(m - m_new)
        p = jnp.exp(s - m_new)
        l = alpha * l + jnp.sum(p, axis=-1, keepdims=True)
        acc = alpha * acc + _dot(p.astype(BF16), v)
        m = m_new
    o_slc = acc / l

    w = win_ref.shape[0] // WIN_SLABS
    wk = _slab(win_ref, g, w, WIN_SLABS).astype(BF16)
    wv = _slab(win_ref, NSA_KV + g, w, WIN_SLABS).astype(BF16)
    sw = _dot_nt(q, wk) + biasw_ref[...]
    valid = _iota((1, w), 1) >= 1
    sw = jnp.where(valid, sw, NEG)
    s_new = jnp.sum(qf * wknew_ref[...], axis=-1, keepdims=True) + bias0
    mw = jnp.maximum(jnp.max(sw, axis=-1, keepdims=True), s_new)
    pw = jnp.where(valid, jnp.exp(sw - mw), 0.0)
    p_new = jnp.exp(s_new - mw)
    den = jnp.sum(pw, axis=-1, keepdims=True) + p_new
    o_win = (_dot(pw.astype(BF16), wv) + p_new * wvnew_ref[...]) / den

    gates = jnp.broadcast_to(gate_ref[...], (SUBLANES, LANES))
    lane = _iota((SUBLANES, LANES), 1)
    row = _iota((SUBLANES, LANES), 0)

    def gate(c):
        return jnp.sum(jnp.where(lane == c * NSA_GROUP + row, gates, 0.0), axis=-1, keepdims=True)

    o = gate(0) * ocmp_ref[...] + gate(1) * o_slc + gate(2) * o_win
    for jh in range(NSA_GROUP):
        o_ref[:, jh * HEAD_DIM:(jh + 1) * HEAD_DIM] = o[jh:jh + 1, :]


def _nsa_sample(page_table, idx, qg, ocmp, gates3, nsa_rows4, win_rows4, cache_blk, state_rows, bias_blocks,
                bias_w, bias0, layer, per_page):
    batch, n_pages = page_table.shape
    n_blk = NSA_N_SEL - 1
    w = state_rows.shape[2] // WIN_SLABS
    gw = NSA_GROUP * HEAD_DIM
    nb = bias_blocks.shape[1]

    def cache_spec(u):
        def index(b, g, pt, ix):
            n = ix[b, g, u]
            return (layer, pt[b, n // per_page] * per_page + n % per_page, 0, 0)
        return pl.BlockSpec((None, None) + cache_blk.shape[2:], index)

    def new_spec(slab):
        return pl.BlockSpec((None, None, 1, HEAD_DIM), lambda b, g, pt, ix: (b, slab + g, 0, 0))

    grid_spec = pltpu.PrefetchScalarGridSpec(
        num_scalar_prefetch=2,
        grid=(batch, NSA_KV),
        in_specs=[
            pl.BlockSpec((None, None, SUBLANES, HEAD_DIM), lambda b, g, pt, ix: (b, g, 0, 0)),
            pl.BlockSpec((None, None, SUBLANES, HEAD_DIM), lambda b, g, pt, ix: (b, g, 0, 0)),
            pl.BlockSpec((None, 1, LANES), lambda b, g, pt, ix: (b, 0, g)),
            new_spec(2 * NSA_KV), new_spec(3 * NSA_KV),
            new_spec(0), new_spec(NSA_KV),
            pl.BlockSpec((None, None) + state_rows.shape[2:], lambda b, g, pt, ix: (layer, b, 0, 0)),
            pl.BlockSpec((None, SUBLANES, w), lambda b, g, pt, ix: (g, 0, 0)),
            pl.BlockSpec((None, SUBLANES, LANES), lambda b, g, pt, ix: (g, 0, 0)),
            pl.BlockSpec((None, nb, SUBLANES, NSA_BLOCK), lambda b, g, pt, ix: (g, 0, 0, 0)),
        ]
        + [cache_spec(u) for u in range(n_blk)],
        out_specs=pl.BlockSpec((None, 1, gw), lambda b, g, pt, ix: (b, 0, g)),
    )
    return pl.pallas_call(
        _nsa_sample_kernel,
        grid_spec=grid_spec,
        out_shape=jax.ShapeDtypeStruct((batch, 1, NSA_HEADS * HEAD_DIM), F32),
        compiler_params=_cparams(("arbitrary", "arbitrary")),
        name="nsa_sample",
    )(page_table, idx, qg, ocmp, gates3, nsa_rows4, nsa_rows4, win_rows4, win_rows4, state_rows,
      bias_w, bias0, bias_blocks, *([cache_blk] * n_blk))


def _rel_bucket(dist):
    exact = REL_BUCKETS // 2
    d = jnp.maximum(dist, 0)
    df = jnp.maximum(d, 1).astype(F32)
    far = exact + (jnp.log(df / exact) / math.log(REL_MAX_DIST / exact) * (REL_BUCKETS - exact)).astype(I32)
    return jnp.where(d < exact, d, jnp.minimum(far, REL_BUCKETS - 1))


def _pad_rows(a, axis, n):
    pad = [(0, 0)] * a.ndim
    pad[axis] = (0, n - a.shape[axis])
    return jnp.pad(a, pad)


def kernel(x_prompt, x_sample, cache_nsa, cache_sb, cache_moba, state_nsa_win, page_table, norm_mix, w_in,
           nsa_cmp_pe, nsa_cmp_alpha, nsa_cmp_w1, nsa_cmp_w2, rel_bias, w_out, norm_ffn, w_up, w_down,
           norm_final):
    batch, seq, d = x_prompt.shape
    dec_batch, dec_seq, _ = x_sample.shape
    depth = w_in.shape[0]
    n_pool, page = cache_nsa.shape[1], cache_nsa.shape[2]
    n_pages = page_table.shape[1]
    past = n_pages * page
    w_buf = state_nsa_win.shape[2]
    assert dec_seq == 1 and seq % ATT_TILE == 0 and ATT_TILE == MOBA_BLOCK
    assert page % NSA_BLOCK == 0 and MOBA_BLOCK % page == 0 and w_buf == NSA_WINDOW
    assert past // NSA_BLOCK >= NSA_N_SEL and past // MOBA_BLOCK >= MOBA_TOPK
    assert dec_batch <= SAMPLE_ROWS and (batch * seq) % ROW_TILE == 0
    assert NSA_SLABS == SUBLANES and SB_SLABS == SUBLANES and MOBA_SLABS == SUBLANES

    o_gate = MIX_WIDTH + KV_COLS
    w_main = w_in[:, :, :o_gate].astype(BF16)
    wg = w_in[:, :, o_gate:].reshape(depth, d, 3, NSA_KV, NSA_GROUP).transpose(0, 1, 3, 2, 4)
    wg = wg.reshape(depth, d, NSA_KV, 3 * NSA_GROUP)
    w_gate = _pad_rows(wg, 3, LANES).reshape(depth, d, NSA_KV * LANES).astype(BF16)
    w_out_b = w_out.astype(BF16)
    w_up_b = w_up.astype(BF16)
    w_down_b = w_down.astype(BF16)
    g_mix = norm_mix.reshape(depth, 1, d)
    g_ffn = norm_ffn.reshape(depth, 1, d)
    g_fin = norm_final.reshape(1, d)
    alpha_b = jnp.broadcast_to(nsa_cmp_alpha[..., None], nsa_cmp_alpha.shape + (HEAD_DIM,))

    def slab_params(p):
        p = jnp.repeat(p, NSA_KV, axis=1)
        return _pad_rows(p, 1, NSA_SLABS).transpose(0, 2, 1, 3)

    alpha_w = slab_params(alpha_b)
    pe_w = slab_params(nsa_cmp_pe)

    max_d = max(seq, past)
    bucket_1d = _rel_bucket(jnp.arange(max_d + 1, dtype=I32))
    thr = jnp.sum(bucket_1d[None, :] < jnp.arange(REL_BUCKETS, dtype=I32)[:, None], axis=1).astype(I32)
    bias_tiles = _bias_tiles(thr, rel_bias.T, seq // ATT_TILE)
    bias_1d = rel_bias[bucket_1d].T
    nsa_1d = bias_1d[:NSA_HEADS]
    moba_1d = bias_1d[NSA_HEADS:]
    nb_p = seq // NSA_BLOCK
    dist_c = jnp.arange(seq, dtype=I32)[:, None] - (jnp.arange(nb_p, dtype=I32) * NSA_BLOCK + NSA_BLOCK - 1)[None, :]
    biasc_p = nsa_1d[:, jnp.maximum(dist_c, 0)].reshape(NSA_KV, NSA_GROUP, seq, nb_p)

    nb_s = past // NSA_BLOCK

    def heads8(a):
        a = a.reshape((NSA_KV, NSA_GROUP) + a.shape[1:])
        return _pad_rows(a, 1, SUBLANES)

    nsa_back = nsa_1d[:, 1:past + 1][:, ::-1]
    moba_back = moba_1d[:, 1:past + 1][:, ::-1]
    biasc_s = heads8(nsa_back[:, NSA_BLOCK - 1::NSA_BLOCK])
    bias_blocks = heads8(nsa_back.reshape(NSA_HEADS, nb_s, NSA_BLOCK)).transpose(0, 2, 1, 3)
    bias_w = heads8(nsa_1d[:, 1:NSA_WINDOW + 1][:, ::-1])
    bias0_nsa = jnp.broadcast_to(heads8(nsa_1d[:, 0])[..., None], (NSA_KV, SUBLANES, LANES))
    bias_pages = moba_back.reshape(MOBA_HEADS, n_pages, 1, page)
    bias0_moba = jnp.broadcast_to(moba_1d[:, 0][:, None, None], (MOBA_HEADS, 1, LANES))

    cache_nsa_rows = cache_nsa.reshape(depth, n_pool, page * NSA_SLABS, HEAD_DIM)
    per_page = page // NSA_BLOCK
    cache_nsa_blk = cache_nsa.reshape(depth, n_pool * per_page, NSA_BLOCK * NSA_SLABS, HEAD_DIM)
    cache_sb_rows = cache_sb.reshape(depth, n_pool, page * SB_SLABS, HEAD_DIM)
    cache_moba_rows = cache_moba.reshape(depth, n_pool, page * MOBA_SLABS, HEAD_DIM)
    state_rows = state_nsa_win.reshape(depth, dec_batch, w_buf * WIN_SLABS, HEAD_DIM)

    xp = x_prompt.reshape(batch * seq, d)
    xs = _pad_rows(x_sample.reshape(dec_batch, d), 0, SAMPLE_ROWS)
    win_keep = min(NSA_WINDOW, seq)
    kv_p = ()
    kv_s = ()
    win_s_out = []

    for l in range(depth):
        last = l == depth - 1
        q, *kv_p, gates, kvb = _in_proj(xp, g_mix, w_main, w_gate, l, depth, IN_ROW_TILE, kv_p)
        mix_nsa = _nsa_prompt(q, kvb, gates, alpha_b, nsa_cmp_pe, nsa_cmp_w1, nsa_cmp_w2,
                              biasc_p, bias_tiles, l, batch, seq)
        mix_sb = _sb_prompt(q, kvb, batch, seq)
        mix_moba = _moba_prompt(q, kvb, bias_tiles, batch, seq)
        xp = _out_proj(xp, mix_nsa, mix_sb, mix_moba, w_out_b, l, ROW_TILE)
        xp = _ffn(xp, g_ffn, w_up_b, w_down_b, g_fin, l, ROW_TILE, last)

        q_s, *kv_s, gates_s, _ = _in_proj(xs, g_mix, w_main, w_gate, l, depth, SAMPLE_ROWS, kv_s)
        nsa_s, win_s, _, moba_s = [a.reshape(depth, -1, HEAD_DIM)[l] for a in kv_s]
        q3 = q_s.astype(F32).reshape(SAMPLE_ROWS, 1, MIX_WIDTH)
        qh = q3[:dec_batch, 0].reshape(dec_batch, N_HEADS, HEAD_DIM)
        qg = _pad_rows(qh[:, :NSA_HEADS].reshape(dec_batch, NSA_KV, NSA_GROUP, HEAD_DIM), 2, SUBLANES)
        q_moba8 = _pad_rows(qh[:, NSA_HEADS + SB_HEADS:], 1, MOBA_SLABS)
        nsa_s4 = nsa_s.reshape(SAMPLE_ROWS, NSA_SLABS, 1, HEAD_DIM)
        win_s4 = win_s.reshape(SAMPLE_ROWS, WIN_SLABS, 1, HEAD_DIM)
        moba_s4 = moba_s.reshape(SAMPLE_ROWS, MOBA_SLABS, 1, HEAD_DIM)
        gates_s3 = gates_s.reshape(SAMPLE_ROWS, 1, 2 * LANES)

        ocmp, sel = _nsa_select(page_table, qg, cache_nsa_rows, alpha_w, pe_w, nsa_cmp_w1, nsa_cmp_w2, biasc_s, l)
        o_nsa = _nsa_sample(page_table, sel[:, :, :, 0], qg, ocmp, gates_s3, nsa_s4, win_s4, cache_nsa_blk,
                            state_rows, bias_blocks, bias_w, bias0_nsa, l, per_page)
        o_sb = _sb_sample(page_table, q3, cache_sb_rows, l)
        top = _moba_select(page_table, q_moba8, cache_moba_rows, l)
        o_moba = _moba_sample(page_table, top[:, :, :MOBA_HEADS, 0], q3, cache_moba_rows, moba_s4,
                              bias_pages, bias0_moba, l)

        def rows16(o):
            return _pad_rows(o.reshape(dec_batch, -1), 0, SAMPLE_ROWS).astype(BF16)

        xs = _out_proj(xs, rows16(o_nsa), rows16(o_sb), rows16(o_moba), w_out_b, l, SAMPLE_ROWS)
        xs = _ffn(xs, g_ffn, w_up_b, w_down_b, g_fin, l, SAMPLE_ROWS, last)
        new_win = win_s[:dec_batch * WIN_SLABS].reshape(dec_batch, 1, 2, NSA_KV, HEAD_DIM)
        win_s_out.append(jnp.concatenate([state_nsa_win[l][:, 1:], new_win], axis=1))

    def rows_p(a, *dims):
        return a.reshape((depth, batch, seq) + dims + (HEAD_DIM,))

    def rows_s(a, *dims):
        return a.reshape((depth, SAMPLE_ROWS, 1) + dims + (HEAD_DIM,))[:, :dec_batch]

    nsa_p, win_p, sb_p, moba_p = kv_p
    nsa_s, win_s, sb_s, moba_s = kv_s
    y_prompt = xp.reshape(batch, seq, d)
    y_sample = xs[:dec_batch].reshape(dec_batch, 1, d)
    return (y_prompt, y_sample,
            rows_p(nsa_p, 4, NSA_KV), rows_s(nsa_s, 4, NSA_KV),
            rows_p(sb_p, 2, SB_HEADS), rows_s(sb_s, 2, SB_HEADS),
            rows_p(moba_p, 2, MOBA_HEADS), rows_s(moba_s, 2, MOBA_HEADS),
            rows_p(win_p, 2, NSA_KV)[:, :, seq - win_keep:], jnp.stack(win_s_out))
```

```python
import functools
import math

import jax
import jax.numpy as jnp
from jax import lax
from jax.experimental import pallas as pl
from jax.experimental.pallas import tpu as pltpu

F32 = jnp.float32
BF16 = jnp.bfloat16
I32 = jnp.int32

HEAD_DIM = 128
NSA_HEADS = 8
SB_HEADS = 4
MOBA_HEADS = 4
NSA_GROUP = 4
NSA_KV = 2
N_HEADS = NSA_HEADS + SB_HEADS + MOBA_HEADS
MIX_WIDTH = N_HEADS * HEAD_DIM
NSA_SLABS = 4 * NSA_KV
WIN_SLABS = 2 * NSA_KV
SB_SLABS = 2 * SB_HEADS
MOBA_SLABS = 2 * MOBA_HEADS
NSA_ROW_COLS = NSA_SLABS * HEAD_DIM
WIN_ROW_COLS = WIN_SLABS * HEAD_DIM
SB_ROW_COLS = SB_SLABS * HEAD_DIM
MOBA_ROW_COLS = MOBA_SLABS * HEAD_DIM
KV_COLS = NSA_ROW_COLS + WIN_ROW_COLS + SB_ROW_COLS + MOBA_ROW_COLS
NSA_BLOCK = 64
NSA_N_SEL = 16
NSA_WINDOW = 512
MOBA_BLOCK = 256
MOBA_TOPK = 3
REL_BUCKETS = 32
REL_MAX_DIST = 2048
RMS_EPS = 1e-6
ATTN_SCALE = HEAD_DIM ** -0.5
NEG = -1e30
FORCED = 1e4

V7X_VMEM_LIMIT_BYTES = 56 * 1024 * 1024
LANES = 128
SUBLANES = 8
COL_TILE = 512
ROW_TILE = 1024
IN_ROW_TILE = 512
ATT_TILE = 256
SAMPLE_ROWS = 16
SLABS_PER_TILE = COL_TILE // HEAD_DIM


def _cparams(sem):
    return pltpu.CompilerParams(dimension_semantics=sem, vmem_limit_bytes=V7X_VMEM_LIMIT_BYTES)


def _dot(a, b):
    return jnp.dot(a, b, preferred_element_type=F32)


def _dot_nt(a, b):
    return lax.dot_general(a, b, (((1,), (1,)), ((), ())), preferred_element_type=F32)


def _split_bf16(a, terms):
    parts = []
    for _ in range(terms):
        p = a.astype(BF16)
        parts.append(p)
        a = a - p.astype(F32)
    return parts


def _dot_split(a, b_bf16, terms=2):
    return sum(_dot(p, b_bf16) for p in _split_bf16(a, terms))


def _iota(shape, dim):
    return lax.broadcasted_iota(I32, shape, dim)


def _rms(x, g):
    return (x * lax.rsqrt(jnp.mean(x * x, axis=-1, keepdims=True) + RMS_EPS)) * g


def _softplus(z):
    return jnp.maximum(z, 0.0) + jnp.log(1.0 + jnp.exp(-jnp.abs(z)))


def _head(a, h):
    return a[:, h * HEAD_DIM:(h + 1) * HEAD_DIM]


def _slab(ref, s, n, slabs):
    return ref[pl.ds(s, n, stride=slabs), :]


N_Q_TILES = MIX_WIDTH // COL_TILE
N_KV_TILES = KV_COLS // COL_TILE
KV_GROUPS = (("nsa", NSA_SLABS, 0), ("win", WIN_SLABS, 2), ("sb", SB_SLABS, 3), ("moba", MOBA_SLABS, 5))


def _in_proj_kernel(x_ref, g_ref, w_ref, wg_ref, *refs):
    q_ref, nsa_ref, win_ref, sb_ref, moba_ref, gate_ref, kvb_ref, h_ref = refs[-8:]
    j = pl.program_id(1)
    tm = x_ref.shape[0]

    @pl.when(j == 0)
    def _():
        h = _rms(x_ref[...], g_ref[...]).astype(BF16)
        h_ref[...] = h
        gate_ref[...] = jax.nn.sigmoid(_dot(h, wg_ref[...]))

    acc = _dot(h_ref[...], w_ref[...])

    @pl.when(j < N_Q_TILES)
    def _():
        q_ref[...] = (acc * ATTN_SCALE).astype(BF16)

    @pl.when(j >= N_Q_TILES)
    def _():
        kvb_ref[...] = acc.astype(BF16)

    for (_, slabs, start), ref in zip(KV_GROUPS, (nsa_ref, win_ref, sb_ref, moba_ref)):
        for tt in range(slabs // SLABS_PER_TILE):
            @pl.when(j == N_Q_TILES + start + tt)
            def _(ref=ref, slabs=slabs, tt=tt):
                for s in range(SLABS_PER_TILE):
                    ref[pl.ds(tt * SLABS_PER_TILE + s, tm, stride=slabs), :] = _head(acc, s)


def _in_proj(x, g, w_main, w_gate, layer, depth, tm, prev_kv):
    m, d = x.shape
    n_tiles = N_Q_TILES + N_KV_TILES
    row_tiles = m // tm

    def clamp_map(lo, count):
        return lambda i, j: (i, jnp.clip(j - lo, 0, count - 1))

    out_shape = [jax.ShapeDtypeStruct((m, MIX_WIDTH), BF16)]
    out_specs = [pl.BlockSpec((tm, COL_TILE), clamp_map(0, N_Q_TILES))]
    for _, slabs, _ in KV_GROUPS:
        out_shape.append(jax.ShapeDtypeStruct((depth * m * slabs, HEAD_DIM), F32))
        out_specs.append(pl.BlockSpec((tm * slabs, HEAD_DIM), lambda i, j: (layer * row_tiles + i, 0)))
    out_shape += [jax.ShapeDtypeStruct((m, 2 * LANES), F32), jax.ShapeDtypeStruct((m, KV_COLS), BF16)]
    out_specs += [pl.BlockSpec((tm, 2 * LANES), lambda i, j: (i, 0)),
                  pl.BlockSpec((tm, COL_TILE), clamp_map(N_Q_TILES, N_KV_TILES))]
    n_in = 4
    return pl.pallas_call(
        _in_proj_kernel,
        grid=(row_tiles, n_tiles),
        in_specs=[
            pl.BlockSpec((tm, d), lambda i, j: (i, 0)),
            pl.BlockSpec((None, 1, d), lambda i, j: (layer, 0, 0)),
            pl.BlockSpec((None, d, COL_TILE), lambda i, j: (layer, 0, j)),
            pl.BlockSpec((None, d, 2 * LANES), lambda i, j: (layer, 0, 0)),
        ] + [pl.BlockSpec(memory_space=pl.ANY)] * len(prev_kv),
        out_specs=out_specs,
        out_shape=out_shape,
        input_output_aliases={n_in + k: 1 + k for k in range(len(prev_kv))},
        scratch_shapes=[pltpu.VMEM((tm, d), BF16)],
        compiler_params=_cparams(("arbitrary", "arbitrary")),
        name="in_proj",
    )(x, g, w_main, w_gate, *prev_kv)


def _out_proj_kernel(x_ref, a_ref, b_ref, c_ref, w_ref, o_ref):
    n_a = a_ref.shape[1]
    n_b = b_ref.shape[1]
    acc = _dot(a_ref[...], w_ref[0:n_a, :])
    acc += _dot(b_ref[...], w_ref[n_a:n_a + n_b, :])
    acc += _dot(c_ref[...], w_ref[n_a + n_b:, :])
    o_ref[...] = x_ref[...] + acc


def _out_proj(x, mix_nsa, mix_sb, mix_moba, w_out, layer, tm):
    m, d = x.shape
    return pl.pallas_call(
        _out_proj_kernel,
        grid=(m // tm, d // COL_TILE),
        in_specs=[
            pl.BlockSpec((tm, COL_TILE), lambda i, j: (i, j)),
            pl.BlockSpec((tm, mix_nsa.shape[1]), lambda i, j: (i, 0)),
            pl.BlockSpec((tm, mix_sb.shape[1]), lambda i, j: (i, 0)),
            pl.BlockSpec((tm, mix_moba.shape[1]), lambda i, j: (i, 0)),
            pl.BlockSpec((None, MIX_WIDTH, COL_TILE), lambda i, j: (layer, 0, j)),
        ],
        out_specs=pl.BlockSpec((tm, COL_TILE), lambda i, j: (i, j)),
        out_shape=jax.ShapeDtypeStruct((m, d), F32),
        compiler_params=_cparams(("arbitrary", "arbitrary")),
        name="out_proj",
    )(x, mix_nsa, mix_sb, mix_moba, w_out)


def _ffn_kernel(x_ref, g_ref, wu_ref, wd_ref, gf_ref, o_ref, h_ref, *, final_norm):
    j = pl.program_id(1)

    @pl.when(j == 0)
    def _():
        x = x_ref[...]
        h_ref[...] = _rms(x, g_ref[...]).astype(BF16)
        o_ref[...] = x

    a = jnp.maximum(_dot(h_ref[...], wu_ref[...]), 0.0)
    o_ref[...] += _dot((a * a).astype(BF16), wd_ref[...])

    if final_norm:
        @pl.when(j == pl.num_programs(1) - 1)
        def _():
            o_ref[...] = _rms(o_ref[...], gf_ref[...])


def _ffn(x, g, w_up, w_down, g_final, layer, tm, final_norm):
    m, d = x.shape
    d_ff = w_up.shape[2]
    return pl.pallas_call(
        functools.partial(_ffn_kernel, final_norm=final_norm),
        grid=(m // tm, d_ff // COL_TILE),
        in_specs=[
            pl.BlockSpec((tm, d), lambda i, j: (i, 0)),
            pl.BlockSpec((None, 1, d), lambda i, j: (layer, 0, 0)),
            pl.BlockSpec((None, d, COL_TILE), lambda i, j: (layer, 0, j)),
            pl.BlockSpec((None, COL_TILE, d), lambda i, j: (layer, j, 0)),
            pl.BlockSpec((1, d), lambda i, j: (0, 0)),
        ],
        out_specs=pl.BlockSpec((tm, d), lambda i, j: (i, 0)),
        out_shape=jax.ShapeDtypeStruct((m, d), F32),
        scratch_shapes=[pltpu.VMEM((tm, d), BF16)],
        compiler_params=_cparams(("arbitrary", "arbitrary")),
        name="ffn",
    )(x, g, w_up, w_down, g_final)


def _bias_grid_kernel(thr_ref, tbl_ref, o_ref, *, offset, step, row_step):
    h = pl.program_id(0)
    shape = o_ref.shape
    dist = offset + step * pl.program_id(1) + row_step * _iota(shape, 0) + _iota(shape, 1)
    acc = jnp.full(shape, tbl_ref[h, 0], F32)
    for k in range(1, REL_BUCKETS):
        acc = jnp.where(dist >= thr_ref[k], tbl_ref[h, k], acc)
    o_ref[...] = acc


def _bias_tiles(thr, tbl_t, nd):
    n_heads = tbl_t.shape[0]
    grid_spec = pltpu.PrefetchScalarGridSpec(
        num_scalar_prefetch=2,
        grid=(n_heads, nd),
        in_specs=[],
        out_specs=pl.BlockSpec((None, None, ATT_TILE, ATT_TILE), lambda h, d, thr, tbl: (h, d, 0, 0)),
    )
    return pl.pallas_call(
        functools.partial(_bias_grid_kernel, offset=0, step=ATT_TILE, row_step=-1),
        grid_spec=grid_spec,
        out_shape=jax.ShapeDtypeStruct((n_heads, nd, ATT_TILE, ATT_TILE), F32),
        compiler_params=_cparams(("arbitrary", "arbitrary")),
        name="bias_tiles",
    )(thr, tbl_t)


def _bias_cmp(thr, tbl_t, nb, seq):
    n_heads = tbl_t.shape[0]
    grid_spec = pltpu.PrefetchScalarGridSpec(
        num_scalar_prefetch=2,
        grid=(n_heads, seq // ATT_TILE),
        in_specs=[],
        out_specs=pl.BlockSpec((None, nb, ATT_TILE), lambda h, i, thr, tbl: (h, 0, i)),
    )
    return pl.pallas_call(
        functools.partial(_bias_grid_kernel, offset=-(NSA_BLOCK - 1), step=ATT_TILE, row_step=-NSA_BLOCK),
        grid_spec=grid_spec,
        out_shape=jax.ShapeDtypeStruct((n_heads, nb, seq), F32),
        compiler_params=_cparams(("arbitrary", "arbitrary")),
        name="bias_cmp",
    )(thr, tbl_t)


def _online_steps(m_s, l_s, acc_s, items):
    weights = []
    for slot, s, mask, _ in items:
        if mask is not None:
            s = jnp.where(mask, s, NEG)
        m_prev = m_s[slot]
        m_new = jnp.maximum(m_prev, jnp.max(s, axis=0, keepdims=True))
        alpha = jnp.exp(m_prev - m_new)
        p = jnp.exp(s - m_new)
        l_s[slot] = alpha * l_s[slot] + jnp.sum(p, axis=0, keepdims=True)
        m_s[slot] = m_new
        weights.append((alpha, p.astype(BF16)))
    for (slot, _, _, vt), (alpha, p) in zip(items, weights):
        acc_s[slot] = alpha * acc_s[slot] + _dot(vt, p)


def _reset_state(m_s, l_s, acc_s):
    m_s[...] = jnp.full(m_s.shape, NEG, F32)
    l_s[...] = jnp.zeros(l_s.shape, F32)
    acc_s[...] = jnp.zeros(acc_s.shape, F32)


def _tile(ref, j, t):
    return ref[pl.ds(pl.multiple_of(j * t, t), t), :]


def _tile_t(ref, j, t):
    return ref[:, pl.ds(pl.multiple_of(j * t, t), t)]


def _transpose_into(dst_ref, src_ref, heads):
    seq = src_ref.shape[0]
    for j in range(seq // ATT_TILE):
        rows = slice(j * ATT_TILE, (j + 1) * ATT_TILE)
        for h in range(heads):
            cols = slice(h * HEAD_DIM, (h + 1) * HEAD_DIM)
            dst_ref[cols, rows] = src_ref[rows, cols].astype(F32).T.astype(BF16)


def _rank_rows(score, n):
    idx = _iota((n, 1), 0)
    rank = jnp.zeros(score.shape, F32)
    for mm in range(n):
        sm = score[mm:mm + 1, :]
        first = (mm < idx).astype(F32)
        rank += jnp.where(sm > score, 1.0, jnp.where(sm == score, first, 0.0))
    return rank


def _store_heads(o_ref, outs_t):
    for h, o_t in enumerate(outs_t):
        o_ref[:, h * HEAD_DIM:(h + 1) * HEAD_DIM] = o_t.T.astype(o_ref.dtype)


def _sb_prompt_kernel(q_ref, k_ref, v_ref, o_ref, vt_s, carry_s, acc_s):
    t = ATT_TILE
    i = pl.program_id(1)

    @pl.when(i == 0)
    def _():
        _transpose_into(vt_s, v_ref, SB_HEADS)

    key = _iota((t, t), 0)
    qry = _iota((t, t), 1)
    u_suffix = (qry >= key).astype(BF16)
    strict = key < qry
    carry_s[...] = jnp.zeros(carry_s.shape, F32)
    acc_s[...] = jnp.zeros(acc_s.shape, F32)

    def step(j, mask):
        kt = _tile(k_ref, j, t)
        vt = _tile_t(vt_s, j, t)
        heads = range(SB_HEADS)
        z = [_dot_nt(_head(kt, h), _head(q_ref, h)) for h in heads]
        sp = [_softplus(z[h]) for h in heads]
        lk = [-sp[h] if mask is None else jnp.where(mask, -sp[h], 0.0) for h in heads]
        parts = [_split_bf16(lk[h], 2) for h in heads]
        incl = [sum(_dot(u_suffix, part) for part in parts[h]) for h in heads]
        weights = []
        for h in heads:
            carry = carry_s[h]
            a = jnp.exp(z[h] - sp[h] + (carry + incl[h] - lk[h]))
            if mask is not None:
                a = jnp.where(mask, a, 0.0)
            weights.append(a.astype(BF16))
            carry_s[h] = carry + incl[h][0:1, :]
        for h in heads:
            acc_s[h] += _dot(vt[h * HEAD_DIM:(h + 1) * HEAD_DIM, :], weights[h])

    step(i, strict)

    def body(d, _):
        step(i - d, None)
        return 0

    lax.fori_loop(1, i + 1, body, 0)
    _store_heads(o_ref, [acc_s[h] for h in range(SB_HEADS)])


def _sb_prompt(q, kvb, batch, seq):
    nq = seq // ATT_TILE
    hw = SB_HEADS * HEAD_DIM
    kv0 = (NSA_ROW_COLS + WIN_ROW_COLS) // hw
    q0 = NSA_HEADS * HEAD_DIM // hw
    return pl.pallas_call(
        _sb_prompt_kernel,
        grid=(batch, nq),
        in_specs=[
            pl.BlockSpec((ATT_TILE, hw), lambda b, i: (b * nq + i, q0)),
            pl.BlockSpec((seq, hw), lambda b, i: (b, kv0)),
            pl.BlockSpec((seq, hw), lambda b, i: (b, kv0 + 1)),
        ],
        out_specs=pl.BlockSpec((ATT_TILE, hw), lambda b, i: (b * nq + i, 0)),
        out_shape=jax.ShapeDtypeStruct((batch * seq, hw), BF16),
        scratch_shapes=[pltpu.VMEM((hw, seq), BF16),
                        pltpu.VMEM((SB_HEADS, 1, ATT_TILE), F32),
                        pltpu.VMEM((SB_HEADS, HEAD_DIM, ATT_TILE), F32)],
        compiler_params=_cparams(("arbitrary", "arbitrary")),
        name="sb_prompt",
    )(q, kvb, kvb)


def _moba_prompt_kernel(q_ref, k_ref, v_ref, bias_ref, o_ref, vt_s, kmean_s, sel_s, m_s, l_s, acc_s):
    t = ATT_TILE
    i = pl.program_id(1)
    seq = k_ref.shape[0]
    nb = seq // t

    @pl.when(i == 0)
    def _():
        _transpose_into(vt_s, v_ref, MOBA_HEADS)
        avg = jnp.where(_iota((nb, seq), 1) // t == _iota((nb, seq), 0), 1.0 / t, 0.0).astype(BF16)
        kmean_s[...] = _dot(avg, k_ref[...])

    nrow = _iota((nb, 1), 0)
    past = nrow < i
    for h in range(MOBA_HEADS):
        q = _head(q_ref, h)
        gs = sum(_dot_nt(part, q) for part in _split_bf16(_head(kmean_s, h), 2))
        rank = _rank_rows(jnp.where(past, gs, NEG), nb)
        sel_s[h] = jnp.where(past & (rank < MOBA_TOPK), 1.0, 0.0)

    _reset_state(m_s, l_s, acc_s)
    causal = _iota((t, t), 0) <= _iota((t, t), 1)

    def step(d, diagonal):
        j = i - d
        kt = _tile(k_ref, j, t)
        vt = _tile_t(vt_s, j, t)
        items = []
        for h in range(MOBA_HEADS):
            mask = causal if diagonal else sel_s[h, pl.ds(j, 1), :] > 0.5
            s = _dot_nt(_head(kt, h), _head(q_ref, h)) + bias_ref[h, d]
            items.append((h, s, mask, vt[h * HEAD_DIM:(h + 1) * HEAD_DIM, :]))
        _online_steps(m_s, l_s, acc_s, items)

    step(0, True)

    def body(d, _):
        step(d, False)
        return 0

    lax.fori_loop(1, i + 1, body, 0)
    _store_heads(o_ref, [acc_s[h] / l_s[h] for h in range(MOBA_HEADS)])


def _moba_prompt(q, kvb, bias_tiles, batch, seq):
    nq = seq // ATT_TILE
    hw = MOBA_HEADS * HEAD_DIM
    kv0 = (NSA_ROW_COLS + WIN_ROW_COLS + SB_ROW_COLS) // hw
    q0 = (NSA_HEADS + SB_HEADS) * HEAD_DIM // hw
    nd = bias_tiles.shape[1]
    nb = seq // MOBA_BLOCK
    return pl.pallas_call(
        _moba_prompt_kernel,
        grid=(batch, nq),
        in_specs=[
            pl.BlockSpec((ATT_TILE, hw), lambda b, i: (b * nq + i, q0)),
            pl.BlockSpec((seq, hw), lambda b, i: (b, kv0)),
            pl.BlockSpec((seq, hw), lambda b, i: (b, kv0 + 1)),
            pl.BlockSpec((MOBA_HEADS, nd, ATT_TILE, ATT_TILE), lambda b, i: (NSA_HEADS // MOBA_HEADS, 0, 0, 0)),
        ],
        out_specs=pl.BlockSpec((ATT_TILE, hw), lambda b, i: (b * nq + i, 0)),
        out_shape=jax.ShapeDtypeStruct((batch * seq, hw), BF16),
        scratch_shapes=[
            pltpu.VMEM((hw, seq), BF16),
            pltpu.VMEM((nb, hw), F32),
            pltpu.VMEM((MOBA_HEADS, nb, ATT_TILE), F32),
            pltpu.VMEM((MOBA_HEADS, 1, ATT_TILE), F32),
            pltpu.VMEM((MOBA_HEADS, 1, ATT_TILE), F32),
            pltpu.VMEM((MOBA_HEADS, HEAD_DIM, ATT_TILE), F32),
        ],
        compiler_params=_cparams(("arbitrary", "arbitrary")),
        name="moba_prompt",
    )(q, kvb, kvb, bias_tiles)


def _compress_hidden(z, w1):
    hid = _dot(z.astype(BF16), w1.astype(BF16))
    return (hid * jax.nn.sigmoid(hid)).astype(BF16)


def _nsa_prompt_kernel(q_ref, ck_ref, cv_ref, sk_ref, sv_ref, wk_ref, wv_ref, alpha_ref, pe_ref,
                       w1_ref, w2_ref, w2t_ref, biasc_ref, bias_ref, gate_ref, o_ref,
                       kc_s, vct_s, svt_s, wvt_s, ocmp_s, m_s, l_s, acc_s):
    t = ATT_TILE
    i = pl.program_id(2)
    seq = ck_ref.shape[0]
    nb = seq // NSA_BLOCK
    per_tile = t // NSA_BLOCK

    @pl.when(i == 0)
    def _():
        _transpose_into(svt_s, sv_ref, 1)
        _transpose_into(wvt_s, wv_ref, 1)
        hid = []
        for c, src in enumerate((ck_ref, cv_ref)):
            x = src[...].astype(F32).reshape(nb, NSA_BLOCK, HEAD_DIM)
            z = jnp.sum((x + pe_ref[c][None]) * alpha_ref[c][None], axis=1)
            hid.append(_compress_hidden(z, w1_ref[c]))
        kc_s[...] = _dot(hid[0], w2_ref[0].astype(BF16)).astype(BF16)
        vct_s[...] = _dot_nt(w2t_ref[1].astype(BF16), hid[1]).astype(BF16)

    tpos = i * t + _iota((1, t), 1)
    nrow = _iota((nb, 1), 0)
    complete = tpos >= nrow * NSA_BLOCK + (NSA_BLOCK - 1)
    kc = kc_s[...]
    vct = vct_s[...]
    imp = jnp.zeros((nb, t), F32)
    for jh in range(NSA_GROUP):
        z = jnp.where(complete, _dot_nt(kc, _head(q_ref, jh)) + biasc_ref[jh], NEG)
        z = z - jnp.max(z, axis=0, keepdims=True)
        e = jnp.where(complete, jnp.exp(z), 0.0)
        p = e / jnp.maximum(jnp.sum(e, axis=0, keepdims=True), 1e-30)
        imp += p
        ocmp_s[jh] = _dot(vct, p.astype(BF16))

    forced = (nrow == tpos // NSA_BLOCK) | (nrow == 0)
    score = jnp.where(forced, FORCED, jnp.where(complete, imp, NEG))
    rank = _rank_rows(score, nb)
    sel = jnp.where((rank < NSA_N_SEL) & (score > 0.5 * NEG), 1.0, 0.0).astype(BF16)

    def key_mask(j):
        expand = (_iota((t, nb), 1) == j * per_tile + _iota((t, nb), 0) // NSA_BLOCK).astype(BF16)
        return _dot(expand, sel) > 0.5

    _reset_state(m_s, l_s, acc_s)
    key = _iota((t, t), 0)
    qry = _iota((t, t), 1)

    def tile_step(d, with_window):
        j = i - d
        sk = _tile(sk_ref, j, t)
        svt = _tile_t(svt_s, j, t)
        slc_mask = key_mask(j)
        if with_window:
            wk = _tile(wk_ref, j, t)
            wvt = _tile_t(wvt_s, j, t)
            dist = d * t + qry - key
            win_mask = (dist >= 0) & (dist < NSA_WINDOW)
            slc_mask = slc_mask & (dist >= 0)
        items = []
        for jh in range(NSA_GROUP):
            qh = _head(q_ref, jh)
            bias = bias_ref[jh, d]
            items.append((jh, _dot_nt(sk, qh) + bias, slc_mask, svt))
            if with_window:
                items.append((NSA_GROUP + jh, _dot_nt(wk, qh) + bias, win_mask, wvt))
        _online_steps(m_s, l_s, acc_s, items)

    tile_step(0, True)
    n_win = NSA_WINDOW // t

    def body_win(d, _):
        tile_step(d, True)
        return 0

    def body_far(d, _):
        tile_step(d, False)
        return 0

    lax.fori_loop(1, jnp.minimum(i, n_win) + 1, body_win, 0)
    lax.fori_loop(n_win + 1, i + 1, body_far, 0)

    gates_t = gate_ref[...].T
    outs = []
    for jh in range(NSA_GROUP):
        g_cmp = gates_t[jh:jh + 1, :]
        g_slc = gates_t[NSA_GROUP + jh:NSA_GROUP + jh + 1, :]
        g_win = gates_t[2 * NSA_GROUP + jh:2 * NSA_GROUP + jh + 1, :]
        outs.append(g_cmp * ocmp_s[jh] + g_slc * (acc_s[jh] / l_s[jh])
                    + g_win * (acc_s[NSA_GROUP + jh] / l_s[NSA_GROUP + jh]))
    _store_heads(o_ref, outs)


def _nsa_prompt(q, kvb, gates, alpha_b, pe, w1, w2, w2t, biasc, bias_tiles, layer, batch, seq):
    nq = seq // ATT_TILE
    nb = seq // NSA_BLOCK
    nd = bias_tiles.shape[1]
    gw = NSA_GROUP * HEAD_DIM
    win0 = NSA_ROW_COLS // HEAD_DIM

    def col(c):
        return pl.BlockSpec((seq, HEAD_DIM), lambda b, g, i: (b, c + g))

    def per_layer(shape):
        return pl.BlockSpec((None,) + shape, lambda b, g, i: (layer,) + (0,) * len(shape))

    return pl.pallas_call(
        _nsa_prompt_kernel,
        grid=(batch, NSA_KV, nq),
        in_specs=[
            pl.BlockSpec((ATT_TILE, gw), lambda b, g, i: (b * nq + i, g)),
            col(0), col(NSA_KV), col(2 * NSA_KV), col(3 * NSA_KV),
            col(win0), col(win0 + NSA_KV),
            per_layer((2, NSA_BLOCK, HEAD_DIM)), per_layer((2, NSA_BLOCK, HEAD_DIM)),
            per_layer((2, HEAD_DIM, HEAD_DIM)), per_layer((2, HEAD_DIM, HEAD_DIM)),
            per_layer((2, HEAD_DIM, HEAD_DIM)),
            pl.BlockSpec((NSA_GROUP, nb, ATT_TILE), lambda b, g, i: (g, 0, i)),
            pl.BlockSpec((NSA_GROUP, nd, ATT_TILE, ATT_TILE), lambda b, g, i: (g, 0, 0, 0)),
            pl.BlockSpec((ATT_TILE, LANES), lambda b, g, i: (b * nq + i, g)),
        ],
        out_specs=pl.BlockSpec((ATT_TILE, gw), lambda b, g, i: (b * nq + i, g)),
        out_shape=jax.ShapeDtypeStruct((batch * seq, NSA_HEADS * HEAD_DIM), BF16),
        scratch_shapes=[
            pltpu.VMEM((nb, HEAD_DIM), BF16),
            pltpu.VMEM((HEAD_DIM, nb), BF16),
            pltpu.VMEM((HEAD_DIM, seq), BF16),
            pltpu.VMEM((HEAD_DIM, seq), BF16),
            pltpu.VMEM((NSA_GROUP, HEAD_DIM, ATT_TILE), F32),
            pltpu.VMEM((2 * NSA_GROUP, 1, ATT_TILE), F32),
            pltpu.VMEM((2 * NSA_GROUP, 1, ATT_TILE), F32),
            pltpu.VMEM((2 * NSA_GROUP, HEAD_DIM, ATT_TILE), F32),
        ],
        compiler_params=_cparams(("arbitrary", "arbitrary", "arbitrary")),
        name="nsa_prompt",
    )(q, kvb, kvb, kvb, kvb, kvb, kvb, alpha_b, pe, w1, w2, w2t, biasc, bias_tiles, gates)


def _compress_mlp(z, w1, w2):
    return _dot(_compress_hidden(z, w1), w2.astype(BF16))


SB_PAGES_PER_STEP = 8
MOBA_PAGES_PER_STEP = 16
NSA_PAGES_PER_STEP = 8


def _sb_sample_kernel(pt_ref, q_ref, *refs):
    del pt_ref
    pages = refs[:SB_PAGES_PER_STEP]
    o_ref, carry_s, acc_s = refs[SB_PAGES_PER_STEP:]
    s = pl.program_id(1)
    hw = SB_HEADS * HEAD_DIM
    page = pages[0].shape[0] // SB_SLABS

    @pl.when(s == 0)
    def _():
        carry_s[...] = jnp.zeros(carry_s.shape, F32)
        acc_s[...] = jnp.zeros(acc_s.shape, F32)

    own = _iota((SUBLANES, hw), 1) // HEAD_DIM == _iota((SUBLANES, hw), 0)
    q_bd = jnp.where(own, q_ref[...], 0.0).astype(BF16)
    u_incl = (_iota((page, page), 0) >= _iota((page, page), 1)).astype(BF16)

    def heads_of(ref, first):
        return jnp.concatenate([_slab(ref, first + h, page, SB_SLABS) for h in range(SB_HEADS)], axis=1).astype(BF16)

    z = [_dot_nt(q_bd, heads_of(ref, 0)) for ref in pages]
    sp = [_softplus(zz) for zz in z]
    incl = [_dot_split(-s_, u_incl) for s_ in sp]
    carry = carry_s[...]
    weights = []
    for zz, s_, inc in zip(z, sp, incl):
        weights.append(jnp.exp(zz - s_ + (carry + inc + s_)).astype(BF16))
        carry = carry + inc[:, 0:1]
    carry_s[...] = carry
    acc_s[...] += sum(_dot(a, heads_of(ref, SB_HEADS)) for a, ref in zip(weights, pages))

    @pl.when(s == pl.num_programs(1) - 1)
    def _():
        o_ref[...] = jnp.sum(jnp.where(own, acc_s[...], 0.0), axis=0, keepdims=True)


def _sb_sample(page_table, q3, cache_rows, layer):
    batch, n_pages = page_table.shape
    hw = SB_HEADS * HEAD_DIM
    steps = n_pages // SB_PAGES_PER_STEP

    def page_map(k):
        return lambda b, s, pt: (layer, pt[b, n_pages - 1 - (s * SB_PAGES_PER_STEP + k)], 0, 0)

    grid_spec = pltpu.PrefetchScalarGridSpec(
        num_scalar_prefetch=1,
        grid=(batch, steps),
        in_specs=[pl.BlockSpec((None, 1, hw), lambda b, s, pt: (b, 0, NSA_HEADS * HEAD_DIM // hw))]
        + [pl.BlockSpec((None, None) + cache_rows.shape[2:], page_map(k)) for k in range(SB_PAGES_PER_STEP)],
        out_specs=pl.BlockSpec((None, 1, hw), lambda b, s, pt: (b, 0, 0)),
        scratch_shapes=[pltpu.VMEM((SUBLANES, 1), F32), pltpu.VMEM((SUBLANES, hw), F32)],
    )
    return pl.pallas_call(
        _sb_sample_kernel,
        grid_spec=grid_spec,
        out_shape=jax.ShapeDtypeStruct((batch, 1, hw), F32),
        compiler_params=_cparams(("arbitrary", "arbitrary")),
        name="sb_sample",
    )(page_table, q3, *([cache_rows] * SB_PAGES_PER_STEP))


def _moba_select_kernel(pt_ref, q_ref, *refs):
    del pt_ref
    pages = refs[:MOBA_PAGES_PER_STEP]
    idx_ref, kmean_s = refs[MOBA_PAGES_PER_STEP:]
    s = pl.program_id(1)
    page = pages[0].shape[0]
    per_block = MOBA_BLOCK // page
    blocks = MOBA_PAGES_PER_STEP // per_block
    rows = blocks * MOBA_SLABS
    nb = kmean_s.shape[0] // MOBA_SLABS

    sums = [jnp.sum(ref[...], axis=0) for ref in pages]
    pad = jnp.zeros((MOBA_SLABS - MOBA_HEADS, HEAD_DIM), F32)
    means = [jnp.concatenate([sum(sums[r * per_block:(r + 1) * per_block]) * (1.0 / MOBA_BLOCK), pad], axis=0)
             for r in range(blocks)]
    kmean_s[pl.ds(pl.multiple_of(s * rows, rows), rows), :] = jnp.concatenate(means, axis=0)

    @pl.when(s == pl.num_programs(1) - 1)
    def _():
        prod = (kmean_s[...].reshape(nb, MOBA_SLABS, HEAD_DIM) * q_ref[...][None]).reshape(nb * MOBA_SLABS, HEAD_DIM)
        ones = jnp.ones((HEAD_DIM, LANES), BF16)
        gs = _dot_split(prod, ones, 3).reshape(nb, MOBA_SLABS, LANES)
        n_idx = _iota((nb, MOBA_SLABS, LANES), 0)
        rank = jnp.zeros((nb, MOBA_SLABS, LANES), F32)
        for mm in range(nb):
            gm = gs[mm][None]
            first = (mm < n_idx).astype(F32)
            rank += jnp.where(gm > gs, 1.0, jnp.where(gm == gs, first, 0.0))
        n_f = n_idx.astype(F32)
        for k in range(MOBA_TOPK):
            idx_ref[k] = jnp.sum(jnp.where(rank == float(k), n_f, 0.0), axis=0).astype(I32)


def _moba_select(page_table, q8, cache, layer):
    batch, n_pages = page_table.shape
    page = cache.shape[2]
    steps = n_pages // MOBA_PAGES_PER_STEP
    nb = n_pages * page // MOBA_BLOCK

    def page_map(k):
        return lambda b, s, pt: (layer, pt[b, s * MOBA_PAGES_PER_STEP + k], 0, 0, 0, 0)

    grid_spec = pltpu.PrefetchScalarGridSpec(
        num_scalar_prefetch=1,
        grid=(batch, steps),
        in_specs=[pl.BlockSpec((None, MOBA_SLABS, HEAD_DIM), lambda b, s, pt: (b, 0, 0))]
        + [pl.BlockSpec((None, None, page, None, MOBA_HEADS, HEAD_DIM), page_map(k))
           for k in range(MOBA_PAGES_PER_STEP)],
        out_specs=pl.BlockSpec((None, MOBA_TOPK, MOBA_SLABS, LANES), lambda b, s, pt: (b, 0, 0, 0)),
        scratch_shapes=[pltpu.VMEM((nb * MOBA_SLABS, HEAD_DIM), F32)],
    )
    return pl.pallas_call(
        _moba_select_kernel,
        grid_spec=grid_spec,
        out_shape=jax.ShapeDtypeStruct((batch, MOBA_TOPK, MOBA_SLABS, LANES), I32),
        compiler_params=_cparams(("arbitrary", "arbitrary")),
        name="moba_select",
    )(page_table, q8, *([cache] * MOBA_PAGES_PER_STEP))


def _moba_sample_kernel(pt_ref, idx_ref, q_ref, knew_ref, vnew_ref, bias0_ref, *refs):
    del pt_ref, idx_ref
    n = (len(refs) - 1) // 2
    page_refs, b_refs = refs[0:n], refs[n:2 * n]
    o_ref = refs[2 * n]
    h = pl.program_id(1)
    q = q_ref[...]
    q8 = jnp.broadcast_to(q, (SUBLANES, HEAD_DIM)).astype(BF16)
    page = page_refs[0].shape[0] // MOBA_SLABS
    s_new = jnp.sum(q * knew_ref[...], axis=-1, keepdims=True) + bias0_ref[:, 0:1]
    s = [(_dot_nt(q8, _slab(ref, h, page, MOBA_SLABS).astype(BF16)) + b_ref[...])[0:1, :]
         for ref, b_ref in zip(page_refs, b_refs)]
    m = s_new
    for s_ in s:
        m = jnp.maximum(m, jnp.max(s_, axis=-1, keepdims=True))
    p_new = jnp.exp(s_new - m)
    p = [jnp.exp(s_ - m) for s_ in s]
    den = p_new + sum(jnp.sum(p_, axis=-1, keepdims=True) for p_ in p)
    acc = p_new * vnew_ref[...]
    for p_, ref in zip(p, page_refs):
        p8 = jnp.broadcast_to(p_, (SUBLANES, page)).astype(BF16)
        acc = acc + _dot(p8, _slab(ref, MOBA_HEADS + h, page, MOBA_SLABS).astype(BF16))[0:1, :]
    o_ref[...] = acc / den


def _moba_sample(page_table, idx, q3, cache_rows, rows4, bias_pages, bias0, layer):
    batch, n_pages = page_table.shape
    page = cache_rows.shape[2] // MOBA_SLABS
    per_block = MOBA_BLOCK // page
    n = MOBA_TOPK * per_block

    def pg(b, h, u, ix):
        return ix[b, u // per_block, h] * per_block + u % per_block

    def page_spec(u):
        return pl.BlockSpec((None, None) + cache_rows.shape[2:],
                            lambda b, h, pt, ix: (layer, pt[b, pg(b, h, u, ix)], 0, 0))

    def bias_spec(u):
        return pl.BlockSpec((None, None, 1, page), lambda b, h, pt, ix: (h, pg(b, h, u, ix), 0, 0))

    grid_spec = pltpu.PrefetchScalarGridSpec(
        num_scalar_prefetch=2,
        grid=(batch, MOBA_HEADS),
        in_specs=[
            pl.BlockSpec((None, 1, HEAD_DIM), lambda b, h, pt, ix: (b, 0, NSA_HEADS + SB_HEADS + h)),
            pl.BlockSpec((None, None, 1, HEAD_DIM), lambda b, h, pt, ix: (b, h, 0, 0)),
            pl.BlockSpec((None, None, 1, HEAD_DIM), lambda b, h, pt, ix: (b, MOBA_HEADS + h, 0, 0)),
            pl.BlockSpec((None, 1, LANES), lambda b, h, pt, ix: (h, 0, 0)),
        ]
        + [page_spec(u) for u in range(n)] + [bias_spec(u) for u in range(n)],
        out_specs=pl.BlockSpec((None, 1, HEAD_DIM), lambda b, h, pt, ix: (b, 0, h)),
    )
    return pl.pallas_call(
        _moba_sample_kernel,
        grid_spec=grid_spec,
        out_shape=jax.ShapeDtypeStruct((batch, 1, MOBA_HEADS * HEAD_DIM), F32),
        compiler_params=_cparams(("arbitrary", "arbitrary")),
        name="moba_sample",
    )(page_table, idx, q3, rows4, rows4, bias0, *([cache_rows] * n), *([bias_pages] * n))


def _nsa_select_kernel(pt_ref, qg_ref, alpha_ref, pe_ref, w1_ref, w2_ref, biasc_ref, *refs):
    del pt_ref
    pages = refs[:NSA_PAGES_PER_STEP]
    ocmp_ref, idx_ref, z_s = refs[NSA_PAGES_PER_STEP:]
    s = pl.program_id(1)
    page = pages[0].shape[0]
    per_page = page // NSA_BLOCK
    rows = NSA_PAGES_PER_STEP * per_page
    nb = z_s.shape[0]

    alpha = alpha_ref[...]
    zs = [jnp.sum(ref[...].reshape((per_page, NSA_BLOCK) + ref.shape[1:]) * alpha[None], axis=1) for ref in pages]
    z_s[pl.ds(pl.multiple_of(s * rows, rows), rows)] = jnp.concatenate(zs, axis=0)

    @pl.when(s == pl.num_programs(1) - 1)
    def _():
        n_lane = _iota((1, nb), 1)
        pe_sum = jnp.sum(alpha * pe_ref[...], axis=0)
        for g in range(NSA_KV):
            kc, vc = [_compress_mlp(z_s[:, c, g, :] + pe_sum[c, g].reshape(1, HEAD_DIM), w1_ref[c], w2_ref[c])
                      .astype(BF16) for c in range(2)]
            lc = _dot_nt(qg_ref[g].astype(BF16), kc) + biasc_ref[g]
            lc = lc - jnp.max(lc, axis=-1, keepdims=True)
            e = jnp.exp(lc)
            p = e / jnp.maximum(jnp.sum(e, axis=-1, keepdims=True), 1e-30)
            ocmp_ref[g] = _dot(p.astype(BF16), vc)
            imp = jnp.sum(p[0:NSA_GROUP], axis=0, keepdims=True)
            by_lane = jnp.broadcast_to(imp, (nb, nb))
            by_row = by_lane.T
            m_idx = _iota((nb, nb), 0)
            n_idx = _iota((nb, nb), 1)
            beats = jnp.where(by_row > by_lane, 1.0,
                              jnp.where((by_row == by_lane) & (m_idx < n_idx), 1.0, 0.0))
            beats = jnp.where(m_idx >= 1, beats, 0.0)
            rank = jnp.sum(beats, axis=0, keepdims=True)
            k_idx = _iota((NSA_N_SEL, nb), 0)
            hit = ((jnp.broadcast_to(rank, (NSA_N_SEL, nb)) == k_idx.astype(F32)) & (n_lane >= 1)
                   & (k_idx < NSA_N_SEL - 2))
            pick = jnp.sum(jnp.where(hit, n_lane.astype(F32), 0.0), axis=-1, keepdims=True)
            idx_ref[g] = jnp.broadcast_to(pick, (NSA_N_SEL, LANES)).astype(I32)


def _nsa_select(page_table, qg, cache, alpha_w, pe_w, w1, w2, biasc_s, layer):
    batch, n_pages = page_table.shape
    page = cache.shape[2]
    steps = n_pages // NSA_PAGES_PER_STEP
    nb = n_pages * page // NSA_BLOCK
    cmp_block = (2, NSA_KV, HEAD_DIM)

    def page_map(k):
        return lambda b, s, pt: (layer, pt[b, s * NSA_PAGES_PER_STEP + k], 0, 0, 0, 0)

    grid_spec = pltpu.PrefetchScalarGridSpec(
        num_scalar_prefetch=1,
        grid=(batch, steps),
        in_specs=[
            pl.BlockSpec((None, NSA_KV, SUBLANES, HEAD_DIM), lambda b, s, pt: (b, 0, 0, 0)),
            pl.BlockSpec((None, NSA_BLOCK) + cmp_block, lambda b, s, pt: (layer, 0, 0, 0, 0)),
            pl.BlockSpec((None, NSA_BLOCK) + cmp_block, lambda b, s, pt: (layer, 0, 0, 0, 0)),
            pl.BlockSpec((None, 2, HEAD_DIM, HEAD_DIM), lambda b, s, pt: (layer, 0, 0, 0)),
            pl.BlockSpec((None, 2, HEAD_DIM, HEAD_DIM), lambda b, s, pt: (layer, 0, 0, 0)),
            pl.BlockSpec((NSA_KV, SUBLANES, nb), lambda b, s, pt: (0, 0, 0)),
        ] + [pl.BlockSpec((None, None, page) + cmp_block, page_map(k)) for k in range(NSA_PAGES_PER_STEP)],
        out_specs=[
            pl.BlockSpec((None, NSA_KV, SUBLANES, HEAD_DIM), lambda b, s, pt: (b, 0, 0, 0)),
            pl.BlockSpec((None, NSA_KV, NSA_N_SEL, LANES), lambda b, s, pt: (b, 0, 0, 0)),
        ],
        scratch_shapes=[pltpu.VMEM((nb,) + cmp_block, F32)],
    )
    return pl.pallas_call(
        _nsa_select_kernel,
        grid_spec=grid_spec,
        out_shape=[jax.ShapeDtypeStruct((batch, NSA_KV, SUBLANES, HEAD_DIM), F32),
                   jax.ShapeDtypeStruct((batch, NSA_KV, NSA_N_SEL, LANES), I32)],
        compiler_params=_cparams(("arbitrary", "arbitrary")),
        name="nsa_select",
    )(page_table, qg, alpha_w, pe_w, w1, w2, biasc_s, *([cache] * NSA_PAGES_PER_STEP))


def _nsa_sample_kernel(pt_ref, idx_ref, qg_ref, ocmp_ref, gate_ref, sknew_ref, svnew_ref, wknew_ref, wvnew_ref,
                       win_ref, biasw_ref, bias0_ref, biasb_ref, *refs):
    del pt_ref
    blk_refs, o_ref = refs[:-1], refs[-1]
    b = pl.program_id(0)
    g = pl.program_id(1)
    qf = qg_ref[...]
    q = qf.astype(BF16)
    bias0 = bias0_ref[:, 0:1]

    s_own = jnp.sum(qf * sknew_ref[...], axis=-1, keepdims=True) + bias0
    s = [_dot_nt(q, _slab(ref, 2 * NSA_KV + g, NSA_BLOCK, NSA_SLABS).astype(BF16)) + biasb_ref[idx_ref[b, g, u]]
         for u, ref in enumerate(blk_refs)]
    m = s_own
    for s_ in s:
        m = jnp.maximum(m, jnp.max(s_, axis=-1, keepdims=True))
    p_own = jnp.exp(s_own - m)
    p = [jnp.exp(s_ - m) for s_ in s]
    l = p_own + sum(jnp.sum(p_, axis=-1, keepdims=True) for p_ in p)
    acc = p_own * svnew_ref[...]
    for p_, ref in zip(p, blk_refs):
        acc = acc + _dot(p_.astype(BF16), _slab(ref, 3 * NSA_KV + g, NSA_BLOCK, NSA_SLABS).astype(BF16))
    o_slc = acc / l

    w = win_ref.shape[0] // WIN_SLABS
    wk = _slab(win_ref, g, w, WIN_SLABS).astype(BF16)
    wv = _slab(win_ref, NSA_KV + g, w, WIN_SLABS).astype(BF16)
    sw = _dot_nt(q, wk) + biasw_ref[...]
    valid = _iota((1, w), 1) >= 1
    sw = jnp.where(valid, sw, NEG)
    s_new = jnp.sum(qf * wknew_ref[...], axis=-1, keepdims=True) + bias0
    mw = jnp.maximum(jnp.max(sw, axis=-1, keepdims=True), s_new)
    pw = jnp.where(valid, jnp.exp(sw - mw), 0.0)
    p_new = jnp.exp(s_new - mw)
    den = jnp.sum(pw, axis=-1, keepdims=True) + p_new
    o_win = (_dot(pw.astype(BF16), wv) + p_new * wvnew_ref[...]) / den

    gates = jnp.broadcast_to(gate_ref[...], (SUBLANES, LANES))
    lane = _iota((SUBLANES, LANES), 1)
    row = _iota((SUBLANES, LANES), 0)

    def gate(c):
        return jnp.sum(jnp.where(lane == c * NSA_GROUP + row, gates, 0.0), axis=-1, keepdims=True)

    o = gate(0) * ocmp_ref[...] + gate(1) * o_slc + gate(2) * o_win
    for jh in range(NSA_GROUP):
        o_ref[:, jh * HEAD_DIM:(jh + 1) * HEAD_DIM] = o[jh:jh + 1, :]


def _nsa_sample(page_table, idx, qg, ocmp, gates3, nsa_rows4, win_rows4, cache_blk, state_rows, bias_blocks,
                bias_w, bias0, layer, per_page):
    batch, n_pages = page_table.shape
    n_blk = NSA_N_SEL - 1
    w = state_rows.shape[2] // WIN_SLABS
    gw = NSA_GROUP * HEAD_DIM
    nb = bias_blocks.shape[1]

    def cache_spec(u):
        def index(b, g, pt, ix):
            n = ix[b, g, u]
            return (layer, pt[b, n // per_page] * per_page + n % per_page, 0, 0)
        return pl.BlockSpec((None, None) + cache_blk.shape[2:], index)

    def new_spec(slab):
        return pl.BlockSpec((None, None, 1, HEAD_DIM), lambda b, g, pt, ix: (b, slab + g, 0, 0))

    grid_spec = pltpu.PrefetchScalarGridSpec(
        num_scalar_prefetch=2,
        grid=(batch, NSA_KV),
        in_specs=[
            pl.BlockSpec((None, None, SUBLANES, HEAD_DIM), lambda b, g, pt, ix: (b, g, 0, 0)),
            pl.BlockSpec((None, None, SUBLANES, HEAD_DIM), lambda b, g, pt, ix: (b, g, 0, 0)),
            pl.BlockSpec((None, 1, LANES), lambda b, g, pt, ix: (b, 0, g)),
            new_spec(2 * NSA_KV), new_spec(3 * NSA_KV),
            new_spec(0), new_spec(NSA_KV),
            pl.BlockSpec((None, None) + state_rows.shape[2:], lambda b, g, pt, ix: (layer, b, 0, 0)),
            pl.BlockSpec((None, SUBLANES, w), lambda b, g, pt, ix: (g, 0, 0)),
            pl.BlockSpec((None, SUBLANES, LANES), lambda b, g, pt, ix: (g, 0, 0)),
            pl.BlockSpec((None, nb, SUBLANES, NSA_BLOCK), lambda b, g, pt, ix: (g, 0, 0, 0)),
        ]
        + [cache_spec(u) for u in range(n_blk)],
        out_specs=pl.BlockSpec((None, 1, gw), lambda b, g, pt, ix: (b, 0, g)),
    )
    return pl.pallas_call(
        _nsa_sample_kernel,
        grid_spec=grid_spec,
        out_shape=jax.ShapeDtypeStruct((batch, 1, NSA_HEADS * HEAD_DIM), F32),
        compiler_params=_cparams(("arbitrary", "arbitrary")),
        name="nsa_sample",
    )(page_table, idx, qg, ocmp, gates3, nsa_rows4, nsa_rows4, win_rows4, win_rows4, state_rows,
      bias_w, bias0, bias_blocks, *([cache_blk] * n_blk))


def _rel_bucket(dist):
    exact = REL_BUCKETS // 2
    d = jnp.maximum(dist, 0)
    df = jnp.maximum(d, 1).astype(F32)
    far = exact + (jnp.log(df / exact) / math.log(REL_MAX_DIST / exact) * (REL_BUCKETS - exact)).astype(I32)
    return jnp.where(d < exact, d, jnp.minimum(far, REL_BUCKETS - 1))


def _pad_rows(a, axis, n):
    pad = [(0, 0)] * a.ndim
    pad[axis] = (0, n - a.shape[axis])
    return jnp.pad(a, pad)


def kernel(x_prompt, x_sample, cache_nsa, cache_sb, cache_moba, state_nsa_win, page_table, norm_mix, w_in,
           nsa_cmp_pe, nsa_cmp_alpha, nsa_cmp_w1, nsa_cmp_w2, rel_bias, w_out, norm_ffn, w_up, w_down,
           norm_final):
    batch, seq, d = x_prompt.shape
    dec_batch, dec_seq, _ = x_sample.shape
    depth = w_in.shape[0]
    n_pool, page = cache_nsa.shape[1], cache_nsa.shape[2]
    n_pages = page_table.shape[1]
    past = n_pages * page
    w_buf = state_nsa_win.shape[2]
    assert dec_seq == 1 and seq % ATT_TILE == 0 and ATT_TILE == MOBA_BLOCK
    assert page % NSA_BLOCK == 0 and MOBA_BLOCK % page == 0 and w_buf == NSA_WINDOW
    assert past // NSA_BLOCK >= NSA_N_SEL and past // MOBA_BLOCK >= MOBA_TOPK
    assert dec_batch <= SAMPLE_ROWS and (batch * seq) % ROW_TILE == 0
    assert NSA_SLABS == SUBLANES and SB_SLABS == SUBLANES and MOBA_SLABS == SUBLANES

    o_gate = MIX_WIDTH + KV_COLS
    w_main = w_in[:, :, :o_gate].astype(BF16)
    wg = w_in[:, :, o_gate:].reshape(depth, d, 3, NSA_KV, NSA_GROUP).transpose(0, 1, 3, 2, 4)
    wg = wg.reshape(depth, d, NSA_KV, 3 * NSA_GROUP)
    w_gate = _pad_rows(wg, 3, LANES).reshape(depth, d, NSA_KV * LANES).astype(BF16)
    w_out_b = w_out.astype(BF16)
    w_up_b = w_up.astype(BF16)
    w_down_b = w_down.astype(BF16)
    g_mix = norm_mix.reshape(depth, 1, d)
    g_ffn = norm_ffn.reshape(depth, 1, d)
    g_fin = norm_final.reshape(1, d)
    alpha_b = jnp.broadcast_to(nsa_cmp_alpha[..., None], nsa_cmp_alpha.shape + (HEAD_DIM,))
    cmp_w2t = nsa_cmp_w2.transpose(0, 1, 3, 2)

    def slab_params(p):
        p = p.transpose(0, 2, 1, 3)[:, :, :, None, :]
        return jnp.broadcast_to(p, p.shape[:3] + (NSA_KV, HEAD_DIM))

    alpha_w = slab_params(alpha_b)
    pe_w = slab_params(nsa_cmp_pe)

    max_d = max(seq, past)
    bucket_1d = _rel_bucket(jnp.arange(max_d + 1, dtype=I32))
    thr = jnp.sum(bucket_1d[None, :] < jnp.arange(REL_BUCKETS, dtype=I32)[:, None], axis=1).astype(I32)
    tbl_t = rel_bias.T
    bias_tiles = _bias_tiles(thr, tbl_t, seq // ATT_TILE)
    biasc_p = _bias_cmp(thr, tbl_t[:NSA_HEADS], seq // NSA_BLOCK, seq)
    bias_1d = rel_bias[bucket_1d].T
    nsa_1d = bias_1d[:NSA_HEADS]
    moba_1d = bias_1d[NSA_HEADS:]

    nb_s = past // NSA_BLOCK

    def heads8(a):
        a = a.reshape((NSA_KV, NSA_GROUP) + a.shape[1:])
        return _pad_rows(a, 1, SUBLANES)

    nsa_back = nsa_1d[:, 1:past + 1][:, ::-1]
    moba_back = moba_1d[:, 1:past + 1][:, ::-1]
    biasc_s = heads8(nsa_back[:, NSA_BLOCK - 1::NSA_BLOCK])
    bias_blocks = heads8(nsa_back.reshape(NSA_HEADS, nb_s, NSA_BLOCK)).transpose(0, 2, 1, 3)
    bias_w = heads8(nsa_1d[:, 1:NSA_WINDOW + 1][:, ::-1])
    bias0_nsa = jnp.broadcast_to(heads8(nsa_1d[:, 0])[..., None], (NSA_KV, SUBLANES, LANES))
    bias_pages = moba_back.reshape(MOBA_HEADS, n_pages, 1, page)
    bias0_moba = jnp.broadcast_to(moba_1d[:, 0][:, None, None], (MOBA_HEADS, 1, LANES))

    per_page = page // NSA_BLOCK
    cache_nsa_blk = cache_nsa.reshape(depth, n_pool * per_page, NSA_BLOCK * NSA_SLABS, HEAD_DIM)
    cache_sb_rows = cache_sb.reshape(depth, n_pool, page * SB_SLABS, HEAD_DIM)
    cache_moba_rows = cache_moba.reshape(depth, n_pool, page * MOBA_SLABS, HEAD_DIM)
    state_rows = state_nsa_win.reshape(depth, dec_batch, w_buf * WIN_SLABS, HEAD_DIM)

    xp = x_prompt.reshape(batch * seq, d)
    xs = _pad_rows(x_sample.reshape(dec_batch, d), 0, SAMPLE_ROWS)
    win_keep = min(NSA_WINDOW, seq)
    kv_p = ()
    kv_s = ()
    win_s_out = []

    for l in range(depth):
        last = l == depth - 1
        q, *kv_p, gates, kvb = _in_proj(xp, g_mix, w_main, w_gate, l, depth, IN_ROW_TILE, kv_p)
        mix_nsa = _nsa_prompt(q, kvb, gates, alpha_b, nsa_cmp_pe, nsa_cmp_w1, nsa_cmp_w2, cmp_w2t,
                              biasc_p, bias_tiles, l, batch, seq)
        mix_sb = _sb_prompt(q, kvb, batch, seq)
        mix_moba = _moba_prompt(q, kvb, bias_tiles, batch, seq)
        xp = _out_proj(xp, mix_nsa, mix_sb, mix_moba, w_out_b, l, ROW_TILE)
        xp = _ffn(xp, g_ffn, w_up_b, w_down_b, g_fin, l, ROW_TILE, last)

        q_s, *kv_s, gates_s, _ = _in_proj(xs, g_mix, w_main, w_gate, l, depth, SAMPLE_ROWS, kv_s)
        nsa_s, win_s, _, moba_s = [a.reshape(depth, -1, HEAD_DIM)[l] for a in kv_s]
        q3 = q_s.astype(F32).reshape(SAMPLE_ROWS, 1, MIX_WIDTH)
        qh = q3[:dec_batch, 0].reshape(dec_batch, N_HEADS, HEAD_DIM)
        qg = _pad_rows(qh[:, :NSA_HEADS].reshape(dec_batch, NSA_KV, NSA_GROUP, HEAD_DIM), 2, SUBLANES)
        q_moba8 = _pad_rows(qh[:, NSA_HEADS + SB_HEADS:], 1, MOBA_SLABS)
        nsa_s4 = nsa_s.reshape(SAMPLE_ROWS, NSA_SLABS, 1, HEAD_DIM)
        win_s4 = win_s.reshape(SAMPLE_ROWS, WIN_SLABS, 1, HEAD_DIM)
        moba_s4 = moba_s.reshape(SAMPLE_ROWS, MOBA_SLABS, 1, HEAD_DIM)
        gates_s3 = gates_s.reshape(SAMPLE_ROWS, 1, 2 * LANES)

        ocmp, sel = _nsa_select(page_table, qg, cache_nsa, alpha_w, pe_w, nsa_cmp_w1, nsa_cmp_w2, biasc_s, l)
        o_nsa = _nsa_sample(page_table, sel[:, :, :, 0], qg, ocmp, gates_s3, nsa_s4, win_s4, cache_nsa_blk,
                            state_rows, bias_blocks, bias_w, bias0_nsa, l, per_page)
        o_sb = _sb_sample(page_table, q3, cache_sb_rows, l)
        top = _moba_select(page_table, q_moba8, cache_moba, l)
        o_moba = _moba_sample(page_table, top[:, :, :MOBA_HEADS, 0], q3, cache_moba_rows, moba_s4,
                              bias_pages, bias0_moba, l)

        def rows16(o):
            return _pad_rows(o.reshape(dec_batch, -1), 0, SAMPLE_ROWS).astype(BF16)

        xs = _out_proj(xs, rows16(o_nsa), rows16(o_sb), rows16(o_moba), w_out_b, l, SAMPLE_ROWS)
        xs = _ffn(xs, g_ffn, w_up_b, w_down_b, g_fin, l, SAMPLE_ROWS, last)
        new_win = win_s[:dec_batch * WIN_SLABS].reshape(dec_batch, 1, 2, NSA_KV, HEAD_DIM)
        win_s_out.append(jnp.concatenate([state_nsa_win[l][:, 1:], new_win], axis=1))

    def rows_p(a, *dims):
        return a.reshape((depth, batch, seq) + dims + (HEAD_DIM,))

    def rows_s(a, *dims):
        return a.reshape((depth, SAMPLE_ROWS, 1) + dims + (HEAD_DIM,))[:, :dec_batch]

    nsa_p, win_p, sb_p, moba_p = kv_p
    nsa_s, win_s, sb_s, moba_s = kv_s
    y_prompt = xp.reshape(batch, seq, d)
    y_sample = xs[:dec_batch].reshape(dec_batch, 1, d)
    return (y_prompt, y_sample,
            rows_p(nsa_p, 4, NSA_KV), rows_s(nsa_s, 4, NSA_KV),
            rows_p(sb_p, 2, SB_HEADS), rows_s(sb_s, 2, SB_HEADS),
            rows_p(moba_p, 2, MOBA_HEADS), rows_s(moba_s, 2, MOBA_HEADS),
            rows_p(win_p, 2, NSA_KV)[:, :, seq - win_keep:], jnp.stack(win_s_out))
```

```python
import functools
import math

import jax
import jax.numpy as jnp
from jax import lax
from jax.experimental import pallas as pl
from jax.experimental.pallas import tpu as pltpu

F32 = jnp.float32
BF16 = jnp.bfloat16
I32 = jnp.int32

HEAD_DIM = 128
NSA_HEADS = 8
SB_HEADS = 4
MOBA_HEADS = 4
NSA_GROUP = 4
NSA_KV = 2
N_HEADS = NSA_HEADS + SB_HEADS + MOBA_HEADS
MIX_WIDTH = N_HEADS * HEAD_DIM
NSA_SLABS = 4 * NSA_KV
WIN_SLABS = 2 * NSA_KV
SB_SLABS = 2 * SB_HEADS
MOBA_SLABS = 2 * MOBA_HEADS
NSA_ROW_COLS = NSA_SLABS * HEAD_DIM
WIN_ROW_COLS = WIN_SLABS * HEAD_DIM
SB_ROW_COLS = SB_SLABS * HEAD_DIM
MOBA_ROW_COLS = MOBA_SLABS * HEAD_DIM
KV_COLS = NSA_ROW_COLS + WIN_ROW_COLS + SB_ROW_COLS + MOBA_ROW_COLS
NSA_BLOCK = 64
NSA_N_SEL = 16
NSA_WINDOW = 512
MOBA_BLOCK = 256
MOBA_TOPK = 3
REL_BUCKETS = 32
REL_MAX_DIST = 2048
RMS_EPS = 1e-6
ATTN_SCALE = HEAD_DIM ** -0.5
NEG = -1e30
FORCED = 1e4

V7X_VMEM_LIMIT_BYTES = 56 * 1024 * 1024
LANES = 128
SUBLANES = 8
COL_TILE = 512
ROW_TILE = 1024
IN_ROW_TILE = 512
ATT_TILE = 256
SAMPLE_ROWS = 16
SLABS_PER_TILE = COL_TILE // HEAD_DIM


def _cparams(sem):
    return pltpu.CompilerParams(dimension_semantics=sem, vmem_limit_bytes=V7X_VMEM_LIMIT_BYTES)


def _dot(a, b):
    return jnp.dot(a, b, preferred_element_type=F32)


def _dot_nt(a, b):
    return lax.dot_general(a, b, (((1,), (1,)), ((), ())), preferred_element_type=F32)


def _split_bf16(a, terms):
    parts = []
    for _ in range(terms):
        p = a.astype(BF16)
        parts.append(p)
        a = a - p.astype(F32)
    return parts


def _dot_split(a, b_bf16, terms=2):
    return sum(_dot(p, b_bf16) for p in _split_bf16(a, terms))


def _iota(shape, dim):
    return lax.broadcasted_iota(I32, shape, dim)


def _rms(x, g):
    return (x * lax.rsqrt(jnp.mean(x * x, axis=-1, keepdims=True) + RMS_EPS)) * g


def _softplus(z):
    return jnp.maximum(z, 0.0) + jnp.log(1.0 + jnp.exp(-jnp.abs(z)))


def _head(a, h):
    return a[:, h * HEAD_DIM:(h + 1) * HEAD_DIM]


def _slab(ref, s, n, slabs):
    return ref[pl.ds(s, n, stride=slabs), :]


N_Q_TILES = MIX_WIDTH // COL_TILE
N_KV_TILES = KV_COLS // COL_TILE
KV_GROUPS = (("nsa", NSA_SLABS, 0), ("win", WIN_SLABS, 2), ("sb", SB_SLABS, 3), ("moba", MOBA_SLABS, 5))


def _in_proj_kernel(x_ref, g_ref, w_ref, wg_ref, *refs):
    q_ref, nsa_ref, win_ref, sb_ref, moba_ref, gate_ref, kvb_ref, h_ref = refs[-8:]
    j = pl.program_id(1)
    tm = x_ref.shape[0]

    @pl.when(j == 0)
    def _():
        h = _rms(x_ref[...], g_ref[...]).astype(BF16)
        h_ref[...] = h
        gate_ref[...] = jax.nn.sigmoid(_dot(h, wg_ref[...]))

    acc = _dot(h_ref[...], w_ref[...])

    @pl.when(j < N_Q_TILES)
    def _():
        q_ref[...] = (acc * ATTN_SCALE).astype(BF16)

    @pl.when(j >= N_Q_TILES)
    def _():
        kvb_ref[...] = acc.astype(BF16)

    for (_, slabs, start), ref in zip(KV_GROUPS, (nsa_ref, win_ref, sb_ref, moba_ref)):
        for tt in range(slabs // SLABS_PER_TILE):
            @pl.when(j == N_Q_TILES + start + tt)
            def _(ref=ref, slabs=slabs, tt=tt):
                for s in range(SLABS_PER_TILE):
                    ref[pl.ds(tt * SLABS_PER_TILE + s, tm, stride=slabs), :] = _head(acc, s)


def _in_proj(x, g, w_main, w_gate, layer, depth, tm, prev_kv):
    m, d = x.shape
    n_tiles = N_Q_TILES + N_KV_TILES
    row_tiles = m // tm

    def clamp_map(lo, count):
        return lambda i, j: (i, jnp.clip(j - lo, 0, count - 1))

    out_shape = [jax.ShapeDtypeStruct((m, MIX_WIDTH), BF16)]
    out_specs = [pl.BlockSpec((tm, COL_TILE), clamp_map(0, N_Q_TILES))]
    for _, slabs, _ in KV_GROUPS:
        out_shape.append(jax.ShapeDtypeStruct((depth * m * slabs, HEAD_DIM), F32))
        out_specs.append(pl.BlockSpec((tm * slabs, HEAD_DIM), lambda i, j: (layer * row_tiles + i, 0)))
    out_shape += [jax.ShapeDtypeStruct((m, 2 * LANES), F32), jax.ShapeDtypeStruct((m, KV_COLS), BF16)]
    out_specs += [pl.BlockSpec((tm, 2 * LANES), lambda i, j: (i, 0)),
                  pl.BlockSpec((tm, COL_TILE), clamp_map(N_Q_TILES, N_KV_TILES))]
    n_in = 4
    return pl.pallas_call(
        _in_proj_kernel,
        grid=(row_tiles, n_tiles),
        in_specs=[
            pl.BlockSpec((tm, d), lambda i, j: (i, 0)),
            pl.BlockSpec((None, 1, d), lambda i, j: (layer, 0, 0)),
            pl.BlockSpec((None, d, COL_TILE), lambda i, j: (layer, 0, j)),
            pl.BlockSpec((None, d, 2 * LANES), lambda i, j: (layer, 0, 0)),
        ] + [pl.BlockSpec(memory_space=pl.ANY)] * len(prev_kv),
        out_specs=out_specs,
        out_shape=out_shape,
        input_output_aliases={n_in + k: 1 + k for k in range(len(prev_kv))},
        scratch_shapes=[pltpu.VMEM((tm, d), BF16)],
        compiler_params=_cparams(("arbitrary", "arbitrary")),
        name="in_proj",
    )(x, g, w_main, w_gate, *prev_kv)


def _out_proj_kernel(x_ref, a_ref, b_ref, c_ref, w_ref, o_ref):
    n_a = a_ref.shape[1]
    n_b = b_ref.shape[1]
    acc = _dot(a_ref[...], w_ref[0:n_a, :])
    acc += _dot(b_ref[...], w_ref[n_a:n_a + n_b, :])
    acc += _dot(c_ref[...], w_ref[n_a + n_b:, :])
    o_ref[...] = x_ref[...] + acc


def _out_proj(x, mix_nsa, mix_sb, mix_moba, w_out, layer, tm):
    m, d = x.shape
    return pl.pallas_call(
        _out_proj_kernel,
        grid=(m // tm, d // COL_TILE),
        in_specs=[
            pl.BlockSpec((tm, COL_TILE), lambda i, j: (i, j)),
            pl.BlockSpec((tm, mix_nsa.shape[1]), lambda i, j: (i, 0)),
            pl.BlockSpec((tm, mix_sb.shape[1]), lambda i, j: (i, 0)),
            pl.BlockSpec((tm, mix_moba.shape[1]), lambda i, j: (i, 0)),
            pl.BlockSpec((None, MIX_WIDTH, COL_TILE), lambda i, j: (layer, 0, j)),
        ],
        out_specs=pl.BlockSpec((tm, COL_TILE), lambda i, j: (i, j)),
        out_shape=jax.ShapeDtypeStruct((m, d), F32),
        compiler_params=_cparams(("arbitrary", "arbitrary")),
        name="out_proj",
    )(x, mix_nsa, mix_sb, mix_moba, w_out)


def _ffn_kernel(x_ref, g_ref, wu_ref, wd_ref, gf_ref, o_ref, h_ref, *, final_norm):
    j = pl.program_id(1)

    @pl.when(j == 0)
    def _():
        x = x_ref[...]
        h_ref[...] = _rms(x, g_ref[...]).astype(BF16)
        o_ref[...] = x

    a = jnp.maximum(_dot(h_ref[...], wu_ref[...]), 0.0)
    o_ref[...] += _dot((a * a).astype(BF16), wd_ref[...])

    if final_norm:
        @pl.when(j == pl.num_programs(1) - 1)
        def _():
            o_ref[...] = _rms(o_ref[...], gf_ref[...])


def _ffn(x, g, w_up, w_down, g_final, layer, tm, final_norm):
    m, d = x.shape
    d_ff = w_up.shape[2]
    return pl.pallas_call(
        functools.partial(_ffn_kernel, final_norm=final_norm),
        grid=(m // tm, d_ff // COL_TILE),
        in_specs=[
            pl.BlockSpec((tm, d), lambda i, j: (i, 0)),
            pl.BlockSpec((None, 1, d), lambda i, j: (layer, 0, 0)),
            pl.BlockSpec((None, d, COL_TILE), lambda i, j: (layer, 0, j)),
            pl.BlockSpec((None, COL_TILE, d), lambda i, j: (layer, j, 0)),
            pl.BlockSpec((1, d), lambda i, j: (0, 0)),
        ],
        out_specs=pl.BlockSpec((tm, d), lambda i, j: (i, 0)),
        out_shape=jax.ShapeDtypeStruct((m, d), F32),
        scratch_shapes=[pltpu.VMEM((tm, d), BF16)],
        compiler_params=_cparams(("arbitrary", "arbitrary")),
        name="ffn",
    )(x, g, w_up, w_down, g_final)


def _bias_grid_kernel(thr_ref, tbl_ref, o_ref, *, offset, step, row_step):
    h = pl.program_id(0)
    shape = o_ref.shape
    dist = offset + step * pl.program_id(1) + row_step * _iota(shape, 0) + _iota(shape, 1)
    acc = jnp.full(shape, tbl_ref[h, 0], F32)
    for k in range(1, REL_BUCKETS):
        acc = jnp.where(dist >= thr_ref[k], tbl_ref[h, k], acc)
    o_ref[...] = acc


def _bias_tiles(thr, tbl_t, nd):
    n_heads = tbl_t.shape[0]
    grid_spec = pltpu.PrefetchScalarGridSpec(
        num_scalar_prefetch=2,
        grid=(n_heads, nd),
        in_specs=[],
        out_specs=pl.BlockSpec((None, None, ATT_TILE, ATT_TILE), lambda h, d, thr, tbl: (h, d, 0, 0)),
    )
    return pl.pallas_call(
        functools.partial(_bias_grid_kernel, offset=0, step=ATT_TILE, row_step=-1),
        grid_spec=grid_spec,
        out_shape=jax.ShapeDtypeStruct((n_heads, nd, ATT_TILE, ATT_TILE), F32),
        compiler_params=_cparams(("arbitrary", "arbitrary")),
        name="bias_tiles",
    )(thr, tbl_t)


def _bias_cmp(thr, tbl_t, nb, seq):
    n_heads = tbl_t.shape[0]
    grid_spec = pltpu.PrefetchScalarGridSpec(
        num_scalar_prefetch=2,
        grid=(n_heads, seq // ATT_TILE),
        in_specs=[],
        out_specs=pl.BlockSpec((None, nb, ATT_TILE), lambda h, i, thr, tbl: (h, 0, i)),
    )
    return pl.pallas_call(
        functools.partial(_bias_grid_kernel, offset=-(NSA_BLOCK - 1), step=ATT_TILE, row_step=-NSA_BLOCK),
        grid_spec=grid_spec,
        out_shape=jax.ShapeDtypeStruct((n_heads, nb, seq), F32),
        compiler_params=_cparams(("arbitrary", "arbitrary")),
        name="bias_cmp",
    )(thr, tbl_t)


def _online_steps(m_s, l_s, acc_s, items):
    weights = []
    for slot, s, mask, _ in items:
        if mask is not None:
            s = jnp.where(mask, s, NEG)
        m_prev = m_s[slot]
        m_new = jnp.maximum(m_prev, jnp.max(s, axis=0, keepdims=True))
        alpha = jnp.exp(m_prev - m_new)
        p = jnp.exp(s - m_new)
        l_s[slot] = alpha * l_s[slot] + jnp.sum(p, axis=0, keepdims=True)
        m_s[slot] = m_new
        weights.append((alpha, p.astype(BF16)))
    for (slot, _, _, vt), (alpha, p) in zip(items, weights):
        acc_s[slot] = alpha * acc_s[slot] + _dot(vt, p)


def _reset_state(m_s, l_s, acc_s):
    m_s[...] = jnp.full(m_s.shape, NEG, F32)
    l_s[...] = jnp.zeros(l_s.shape, F32)
    acc_s[...] = jnp.zeros(acc_s.shape, F32)


def _tile(ref, j, t):
    return ref[pl.ds(pl.multiple_of(j * t, t), t), :]


def _tile_t(ref, j, t):
    return ref[:, pl.ds(pl.multiple_of(j * t, t), t)]


def _transpose_into(dst_ref, src_ref, heads):
    seq = src_ref.shape[0]
    for j in range(seq // ATT_TILE):
        rows = slice(j * ATT_TILE, (j + 1) * ATT_TILE)
        for h in range(heads):
            cols = slice(h * HEAD_DIM, (h + 1) * HEAD_DIM)
            dst_ref[cols, rows] = src_ref[rows, cols].astype(F32).T.astype(BF16)


def _rank_rows(score, n):
    idx = _iota((n, 1), 0)
    rank = jnp.zeros(score.shape, F32)
    for mm in range(n):
        sm = score[mm:mm + 1, :]
        first = (mm < idx).astype(F32)
        rank += jnp.where(sm > score, 1.0, jnp.where(sm == score, first, 0.0))
    return rank


def _store_heads(o_ref, outs_t):
    for h, o_t in enumerate(outs_t):
        o_ref[:, h * HEAD_DIM:(h + 1) * HEAD_DIM] = o_t.T.astype(o_ref.dtype)


def _sb_prompt_kernel(q_ref, k_ref, v_ref, o_ref, vt_s, carry_s, acc_s):
    t = ATT_TILE
    i = pl.program_id(1)

    @pl.when(i == 0)
    def _():
        _transpose_into(vt_s, v_ref, SB_HEADS)

    key = _iota((t, t), 0)
    qry = _iota((t, t), 1)
    u_suffix = (qry >= key).astype(BF16)
    strict = key < qry
    carry_s[...] = jnp.zeros(carry_s.shape, F32)
    acc_s[...] = jnp.zeros(acc_s.shape, F32)

    def step(j, mask):
        kt = _tile(k_ref, j, t)
        vt = _tile_t(vt_s, j, t)
        heads = range(SB_HEADS)
        z = [_dot_nt(_head(kt, h), _head(q_ref, h)) for h in heads]
        sp = [_softplus(z[h]) for h in heads]
        lk = [-sp[h] if mask is None else jnp.where(mask, -sp[h], 0.0) for h in heads]
        parts = [_split_bf16(lk[h], 2) for h in heads]
        incl = [sum(_dot(u_suffix, part) for part in parts[h]) for h in heads]
        weights = []
        for h in heads:
            carry = carry_s[h]
            a = jnp.exp(z[h] - sp[h] + (carry + incl[h] - lk[h]))
            if mask is not None:
                a = jnp.where(mask, a, 0.0)
            weights.append(a.astype(BF16))
            carry_s[h] = carry + incl[h][0:1, :]
        for h in heads:
            acc_s[h] += _dot(vt[h * HEAD_DIM:(h + 1) * HEAD_DIM, :], weights[h])

    step(i, strict)

    def body(d, _):
        step(i - d, None)
        return 0

    lax.fori_loop(1, i + 1, body, 0)
    _store_heads(o_ref, [acc_s[h] for h in range(SB_HEADS)])


def _sb_prompt(q, kvb, batch, seq):
    nq = seq // ATT_TILE
    hw = SB_HEADS * HEAD_DIM
    kv0 = (NSA_ROW_COLS + WIN_ROW_COLS) // hw
    q0 = NSA_HEADS * HEAD_DIM // hw
    return pl.pallas_call(
        _sb_prompt_kernel,
        grid=(batch, nq),
        in_specs=[
            pl.BlockSpec((ATT_TILE, hw), lambda b, i: (b * nq + i, q0)),
            pl.BlockSpec((seq, hw), lambda b, i: (b, kv0)),
            pl.BlockSpec((seq, hw), lambda b, i: (b, kv0 + 1)),
        ],
        out_specs=pl.BlockSpec((ATT_TILE, hw), lambda b, i: (b * nq + i, 0)),
        out_shape=jax.ShapeDtypeStruct((batch * seq, hw), BF16),
        scratch_shapes=[pltpu.VMEM((hw, seq), BF16),
                        pltpu.VMEM((SB_HEADS, 1, ATT_TILE), F32),
                        pltpu.VMEM((SB_HEADS, HEAD_DIM, ATT_TILE), F32)],
        compiler_params=_cparams(("arbitrary", "arbitrary")),
        name="sb_prompt",
    )(q, kvb, kvb)


def _moba_prompt_kernel(q_ref, k_ref, v_ref, bias_ref, o_ref, vt_s, kmean_s, sel_s, m_s, l_s, acc_s):
    t = ATT_TILE
    i = pl.program_id(1)
    seq = k_ref.shape[0]
    nb = seq // t

    @pl.when(i == 0)
    def _():
        _transpose_into(vt_s, v_ref, MOBA_HEADS)
        avg = jnp.where(_iota((nb, seq), 1) // t == _iota((nb, seq), 0), 1.0 / t, 0.0).astype(BF16)
        kmean_s[...] = _dot(avg, k_ref[...])

    nrow = _iota((nb, 1), 0)
    past = nrow < i
    for h in range(MOBA_HEADS):
        q = _head(q_ref, h)
        gs = sum(_dot_nt(part, q) for part in _split_bf16(_head(kmean_s, h), 2))
        rank = _rank_rows(jnp.where(past, gs, NEG), nb)
        sel_s[h] = jnp.where(past & (rank < MOBA_TOPK), 1.0, 0.0)

    _reset_state(m_s, l_s, acc_s)
    causal = _iota((t, t), 0) <= _iota((t, t), 1)

    def step(d, diagonal):
        j = i - d
        kt = _tile(k_ref, j, t)
        vt = _tile_t(vt_s, j, t)
        items = []
        for h in range(MOBA_HEADS):
            mask = causal if diagonal else sel_s[h, pl.ds(j, 1), :] > 0.5
            s = _dot_nt(_head(kt, h), _head(q_ref, h)) + bias_ref[h, d]
            items.append((h, s, mask, vt[h * HEAD_DIM:(h + 1) * HEAD_DIM, :]))
        _online_steps(m_s, l_s, acc_s, items)

    step(0, True)

    def body(d, _):
        step(d, False)
        return 0

    lax.fori_loop(1, i + 1, body, 0)
    _store_heads(o_ref, [acc_s[h] / l_s[h] for h in range(MOBA_HEADS)])


def _moba_prompt(q, kvb, bias_tiles, batch, seq):
    nq = seq // ATT_TILE
    hw = MOBA_HEADS * HEAD_DIM
    kv0 = (NSA_ROW_COLS + WIN_ROW_COLS + SB_ROW_COLS) // hw
    q0 = (NSA_HEADS + SB_HEADS) * HEAD_DIM // hw
    nd = bias_tiles.shape[1]
    nb = seq // MOBA_BLOCK
    return pl.pallas_call(
        _moba_prompt_kernel,
        grid=(batch, nq),
        in_specs=[
            pl.BlockSpec((ATT_TILE, hw), lambda b, i: (b * nq + i, q0)),
            pl.BlockSpec((seq, hw), lambda b, i: (b, kv0)),
            pl.BlockSpec((seq, hw), lambda b, i: (b, kv0 + 1)),
            pl.BlockSpec((MOBA_HEADS, nd, ATT_TILE, ATT_TILE), lambda b, i: (NSA_HEADS // MOBA_HEADS, 0, 0, 0)),
        ],
        out_specs=pl.BlockSpec((ATT_TILE, hw), lambda b, i: (b * nq + i, 0)),
        out_shape=jax.ShapeDtypeStruct((batch * seq, hw), BF16),
        scratch_shapes=[
            pltpu.VMEM((hw, seq), BF16),
            pltpu.VMEM((nb, hw), F32),
            pltpu.VMEM((MOBA_HEADS, nb, ATT_TILE), F32),
            pltpu.VMEM((MOBA_HEADS, 1, ATT_TILE), F32),
            pltpu.VMEM((MOBA_HEADS, 1, ATT_TILE), F32),
            pltpu.VMEM((MOBA_HEADS, HEAD_DIM, ATT_TILE), F32),
        ],
        compiler_params=_cparams(("arbitrary", "arbitrary")),
        name="moba_prompt",
    )(q, kvb, kvb, bias_tiles)


def _compress_hidden(z, w1):
    hid = _dot(z.astype(BF16), w1.astype(BF16))
    return (hid * jax.nn.sigmoid(hid)).astype(BF16)


def _nsa_prompt_kernel(q_ref, ck_ref, cv_ref, sk_ref, sv_ref, wk_ref, wv_ref, alpha_ref, pe_ref,
                       w1_ref, w2_ref, w2t_ref, biasc_ref, bias_ref, gate_ref, o_ref,
                       kc_s, vct_s, svt_s, wvt_s, ocmp_s, m_s, l_s, acc_s):
    t = ATT_TILE
    i = pl.program_id(2)
    seq = ck_ref.shape[0]
    nb = seq // NSA_BLOCK
    per_tile = t // NSA_BLOCK

    @pl.when(i == 0)
    def _():
        _transpose_into(svt_s, sv_ref, 1)
        _transpose_into(wvt_s, wv_ref, 1)
        hid = []
        for c, src in enumerate((ck_ref, cv_ref)):
            x = src[...].astype(F32).reshape(nb, NSA_BLOCK, HEAD_DIM)
            z = jnp.sum((x + pe_ref[c][None]) * alpha_ref[c][None], axis=1)
            hid.append(_compress_hidden(z, w1_ref[c]))
        kc_s[...] = _dot(hid[0], w2_ref[0].astype(BF16)).astype(BF16)
        vct_s[...] = _dot_nt(w2t_ref[1].astype(BF16), hid[1]).astype(BF16)

    tpos = i * t + _iota((1, t), 1)
    nrow = _iota((nb, 1), 0)
    complete = tpos >= nrow * NSA_BLOCK + (NSA_BLOCK - 1)
    kc = kc_s[...]
    vct = vct_s[...]
    imp = jnp.zeros((nb, t), F32)
    for jh in range(NSA_GROUP):
        z = jnp.where(complete, _dot_nt(kc, _head(q_ref, jh)) + biasc_ref[jh], NEG)
        z = z - jnp.max(z, axis=0, keepdims=True)
        e = jnp.where(complete, jnp.exp(z), 0.0)
        p = e / jnp.maximum(jnp.sum(e, axis=0, keepdims=True), 1e-30)
        imp += p
        ocmp_s[jh] = _dot(vct, p.astype(BF16))

    forced = (nrow == tpos // NSA_BLOCK) | (nrow == 0)
    score = jnp.where(forced, FORCED, jnp.where(complete, imp, NEG))
    rank = _rank_rows(score, nb)
    sel = jnp.where((rank < NSA_N_SEL) & (score > 0.5 * NEG), 1.0, 0.0).astype(BF16)

    def key_mask(j):
        expand = (_iota((t, nb), 1) == j * per_tile + _iota((t, nb), 0) // NSA_BLOCK).astype(BF16)
        return _dot(expand, sel) > 0.5

    _reset_state(m_s, l_s, acc_s)
    key = _iota((t, t), 0)
    qry = _iota((t, t), 1)

    def tile_step(d, with_window):
        j = i - d
        sk = _tile(sk_ref, j, t)
        svt = _tile_t(svt_s, j, t)
        slc_mask = key_mask(j)
        if with_window:
            wk = _tile(wk_ref, j, t)
            wvt = _tile_t(wvt_s, j, t)
            dist = d * t + qry - key
            win_mask = (dist >= 0) & (dist < NSA_WINDOW)
            slc_mask = slc_mask & (dist >= 0)
        items = []
        for jh in range(NSA_GROUP):
            qh = _head(q_ref, jh)
            bias = bias_ref[jh, d]
            items.append((jh, _dot_nt(sk, qh) + bias, slc_mask, svt))
            if with_window:
                items.append((NSA_GROUP + jh, _dot_nt(wk, qh) + bias, win_mask, wvt))
        _online_steps(m_s, l_s, acc_s, items)

    tile_step(0, True)
    n_win = NSA_WINDOW // t

    def body_win(d, _):
        tile_step(d, True)
        return 0

    def body_far(d, _):
        tile_step(d, False)
        return 0

    lax.fori_loop(1, jnp.minimum(i, n_win) + 1, body_win, 0)
    lax.fori_loop(n_win + 1, i + 1, body_far, 0)

    gates_t = gate_ref[...].T
    outs = []
    for jh in range(NSA_GROUP):
        g_cmp = gates_t[jh:jh + 1, :]
        g_slc = gates_t[NSA_GROUP + jh:NSA_GROUP + jh + 1, :]
        g_win = gates_t[2 * NSA_GROUP + jh:2 * NSA_GROUP + jh + 1, :]
        outs.append(g_cmp * ocmp_s[jh] + g_slc * (acc_s[jh] / l_s[jh])
                    + g_win * (acc_s[NSA_GROUP + jh] / l_s[NSA_GROUP + jh]))
    _store_heads(o_ref, outs)


def _nsa_prompt(q, kvb, gates, alpha_b, pe, w1, w2, w2t, biasc, bias_tiles, layer, batch, seq):
    nq = seq // ATT_TILE
    nb = seq // NSA_BLOCK
    nd = bias_tiles.shape[1]
    gw = NSA_GROUP * HEAD_DIM
    win0 = NSA_ROW_COLS // HEAD_DIM

    def col(c):
        return pl.BlockSpec((seq, HEAD_DIM), lambda b, g, i: (b, c + g))

    def per_layer(shape):
        return pl.BlockSpec((None,) + shape, lambda b, g, i: (layer,) + (0,) * len(shape))

    return pl.pallas_call(
        _nsa_prompt_kernel,
        grid=(batch, NSA_KV, nq),
        in_specs=[
            pl.BlockSpec((ATT_TILE, gw), lambda b, g, i: (b * nq + i, g)),
            col(0), col(NSA_KV), col(2 * NSA_KV), col(3 * NSA_KV),
            col(win0), col(win0 + NSA_KV),
            per_layer((2, NSA_BLOCK, HEAD_DIM)), per_layer((2, NSA_BLOCK, HEAD_DIM)),
            per_layer((2, HEAD_DIM, HEAD_DIM)), per_layer((2, HEAD_DIM, HEAD_DIM)),
            per_layer((2, HEAD_DIM, HEAD_DIM)),
            pl.BlockSpec((NSA_GROUP, nb, ATT_TILE), lambda b, g, i: (g, 0, i)),
            pl.BlockSpec((NSA_GROUP, nd, ATT_TILE, ATT_TILE), lambda b, g, i: (g, 0, 0, 0)),
            pl.BlockSpec((ATT_TILE, LANES), lambda b, g, i: (b * nq + i, g)),
        ],
        out_specs=pl.BlockSpec((ATT_TILE, gw), lambda b, g, i: (b * nq + i, g)),
        out_shape=jax.ShapeDtypeStruct((batch * seq, NSA_HEADS * HEAD_DIM), BF16),
        scratch_shapes=[
            pltpu.VMEM((nb, HEAD_DIM), BF16),
            pltpu.VMEM((HEAD_DIM, nb), BF16),
            pltpu.VMEM((HEAD_DIM, seq), BF16),
            pltpu.VMEM((HEAD_DIM, seq), BF16),
            pltpu.VMEM((NSA_GROUP, HEAD_DIM, ATT_TILE), F32),
            pltpu.VMEM((2 * NSA_GROUP, 1, ATT_TILE), F32),
            pltpu.VMEM((2 * NSA_GROUP, 1, ATT_TILE), F32),
            pltpu.VMEM((2 * NSA_GROUP, HEAD_DIM, ATT_TILE), F32),
        ],
        compiler_params=_cparams(("arbitrary", "arbitrary", "arbitrary")),
        name="nsa_prompt",
    )(q, kvb, kvb, kvb, kvb, kvb, kvb, alpha_b, pe, w1, w2, w2t, biasc, bias_tiles, gates)


def _compress_mlp(z, w1, w2):
    return _dot(_compress_hidden(z, w1), w2.astype(BF16))


SB_PAGES_PER_STEP = 16
MOBA_PAGES_PER_STEP = 16
NSA_PAGES_PER_STEP = 8


def _sb_sample_kernel(pt_ref, q_ref, *refs):
    del pt_ref
    pages = refs[:SB_PAGES_PER_STEP]
    o_ref, carry_s, acc_s = refs[SB_PAGES_PER_STEP:]
    s = pl.program_id(1)
    hw = SB_HEADS * HEAD_DIM
    page = pages[0].shape[0] // SB_SLABS

    @pl.when(s == 0)
    def _():
        carry_s[...] = jnp.zeros(carry_s.shape, F32)
        acc_s[...] = jnp.zeros(acc_s.shape, F32)

    own = _iota((SUBLANES, hw), 1) // HEAD_DIM == _iota((SUBLANES, hw), 0)
    q_bd = jnp.where(own, q_ref[...], 0.0).astype(BF16)
    u_incl = (_iota((page, page), 0) >= _iota((page, page), 1)).astype(BF16)

    def heads_of(ref, first):
        return jnp.concatenate([_slab(ref, first + h, page, SB_SLABS) for h in range(SB_HEADS)], axis=1).astype(BF16)

    z = [_dot_nt(q_bd, heads_of(ref, 0)) for ref in pages]
    sp = [_softplus(zz) for zz in z]
    incl = [_dot_split(-s_, u_incl) for s_ in sp]
    carry = carry_s[...]
    weights = []
    for zz, s_, inc in zip(z, sp, incl):
        weights.append(jnp.exp(zz - s_ + (carry + inc + s_)).astype(BF16))
        carry = carry + inc[:, 0:1]
    carry_s[...] = carry
    acc_s[...] += sum(_dot(a, heads_of(ref, SB_HEADS)) for a, ref in zip(weights, pages))

    @pl.when(s == pl.num_programs(1) - 1)
    def _():
        o_ref[...] = jnp.sum(jnp.where(own, acc_s[...], 0.0), axis=0, keepdims=True)


def _sb_sample(page_table, q3, cache_rows, layer):
    batch, n_pages = page_table.shape
    hw = SB_HEADS * HEAD_DIM
    steps = n_pages // SB_PAGES_PER_STEP

    def page_map(k):
        return lambda b, s, pt: (layer, pt[b, n_pages - 1 - (s * SB_PAGES_PER_STEP + k)], 0, 0)

    grid_spec = pltpu.PrefetchScalarGridSpec(
        num_scalar_prefetch=1,
        grid=(batch, steps),
        in_specs=[pl.BlockSpec((None, 1, hw), lambda b, s, pt: (b, 0, NSA_HEADS * HEAD_DIM // hw))]
        + [pl.BlockSpec((None, None) + cache_rows.shape[2:], page_map(k)) for k in range(SB_PAGES_PER_STEP)],
        out_specs=pl.BlockSpec((None, 1, hw), lambda b, s, pt: (b, 0, 0)),
        scratch_shapes=[pltpu.VMEM((SUBLANES, 1), F32), pltpu.VMEM((SUBLANES, hw), F32)],
    )
    return pl.pallas_call(
        _sb_sample_kernel,
        grid_spec=grid_spec,
        out_shape=jax.ShapeDtypeStruct((batch, 1, hw), F32),
        compiler_params=_cparams(("arbitrary", "arbitrary")),
        name="sb_sample",
    )(page_table, q3, *([cache_rows] * SB_PAGES_PER_STEP))


def _moba_select_kernel(pt_ref, q_ref, *refs):
    del pt_ref
    pages = refs[:MOBA_PAGES_PER_STEP]
    idx_ref, kmean_s = refs[MOBA_PAGES_PER_STEP:]
    s = pl.program_id(1)
    page = pages[0].shape[0]
    per_block = MOBA_BLOCK // page
    blocks = MOBA_PAGES_PER_STEP // per_block
    rows = blocks * MOBA_SLABS
    nb = kmean_s.shape[0] // MOBA_SLABS

    sums = [jnp.sum(ref[...], axis=0) for ref in pages]
    pad = jnp.zeros((MOBA_SLABS - MOBA_HEADS, HEAD_DIM), F32)
    means = [jnp.concatenate([sum(sums[r * per_block:(r + 1) * per_block]) * (1.0 / MOBA_BLOCK), pad], axis=0)
             for r in range(blocks)]
    kmean_s[pl.ds(pl.multiple_of(s * rows, rows), rows), :] = jnp.concatenate(means, axis=0)

    @pl.when(s == pl.num_programs(1) - 1)
    def _():
        prod = (kmean_s[...].reshape(nb, MOBA_SLABS, HEAD_DIM) * q_ref[...][None]).reshape(nb * MOBA_SLABS, HEAD_DIM)
        ones = jnp.ones((HEAD_DIM, LANES), BF16)
        gs = _dot_split(prod, ones, 3).reshape(nb, MOBA_SLABS, LANES)
        n_idx = _iota((nb, MOBA_SLABS, LANES), 0)
        rank = jnp.zeros((nb, MOBA_SLABS, LANES), F32)
        for mm in range(nb):
            gm = gs[mm][None]
            first = (mm < n_idx).astype(F32)
            rank += jnp.where(gm > gs, 1.0, jnp.where(gm == gs, first, 0.0))
        n_f = n_idx.astype(F32)
        for k in range(MOBA_TOPK):
            idx_ref[k] = jnp.sum(jnp.where(rank == float(k), n_f, 0.0), axis=0).astype(I32)


def _moba_select(page_table, q8, cache, layer):
    batch, n_pages = page_table.shape
    page = cache.shape[2]
    steps = n_pages // MOBA_PAGES_PER_STEP
    nb = n_pages * page // MOBA_BLOCK

    def page_map(k):
        return lambda b, s, pt: (layer, pt[b, s * MOBA_PAGES_PER_STEP + k], 0, 0, 0, 0)

    grid_spec = pltpu.PrefetchScalarGridSpec(
        num_scalar_prefetch=1,
        grid=(batch, steps),
        in_specs=[pl.BlockSpec((None, MOBA_SLABS, HEAD_DIM), lambda b, s, pt: (b, 0, 0))]
        + [pl.BlockSpec((None, None, page, None, MOBA_HEADS, HEAD_DIM), page_map(k))
           for k in range(MOBA_PAGES_PER_STEP)],
        out_specs=pl.BlockSpec((None, MOBA_TOPK, MOBA_SLABS, LANES), lambda b, s, pt: (b, 0, 0, 0)),
        scratch_shapes=[pltpu.VMEM((nb * MOBA_SLABS, HEAD_DIM), F32)],
    )
    return pl.pallas_call(
        _moba_select_kernel,
        grid_spec=grid_spec,
        out_shape=jax.ShapeDtypeStruct((batch, MOBA_TOPK, MOBA_SLABS, LANES), I32),
        compiler_params=_cparams(("arbitrary", "arbitrary")),
        name="moba_select",
    )(page_table, q8, *([cache] * MOBA_PAGES_PER_STEP))


def _moba_sample_kernel(pt_ref, idx_ref, q_ref, knew_ref, vnew_ref, bias0_ref, *refs):
    del pt_ref, idx_ref
    n = (len(refs) - 1) // 2
    page_refs, b_refs = refs[0:n], refs[n:2 * n]
    o_ref = refs[2 * n]
    h = pl.program_id(1)
    q = q_ref[...]
    q8 = jnp.broadcast_to(q, (SUBLANES, HEAD_DIM)).astype(BF16)
    page = page_refs[0].shape[0] // MOBA_SLABS
    s_new = jnp.sum(q * knew_ref[...], axis=-1, keepdims=True) + bias0_ref[:, 0:1]
    s = [(_dot_nt(q8, _slab(ref, h, page, MOBA_SLABS).astype(BF16)) + b_ref[...])[0:1, :]
         for ref, b_ref in zip(page_refs, b_refs)]
    m = s_new
    for s_ in s:
        m = jnp.maximum(m, jnp.max(s_, axis=-1, keepdims=True))
    p_new = jnp.exp(s_new - m)
    p = [jnp.exp(s_ - m) for s_ in s]
    den = p_new + sum(jnp.sum(p_, axis=-1, keepdims=True) for p_ in p)
    acc = p_new * vnew_ref[...]
    for p_, ref in zip(p, page_refs):
        p8 = jnp.broadcast_to(p_, (SUBLANES, page)).astype(BF16)
        acc = acc + _dot(p8, _slab(ref, MOBA_HEADS + h, page, MOBA_SLABS).astype(BF16))[0:1, :]
    o_ref[...] = acc / den


def _moba_sample(page_table, idx, q3, cache_rows, rows4, bias_pages, bias0, layer):
    batch, n_pages = page_table.shape
    page = cache_rows.shape[2] // MOBA_SLABS
    per_block = MOBA_BLOCK // page
    n = MOBA_TOPK * per_block

    def pg(b, h, u, ix):
        return ix[b, u // per_block, h] * per_block + u % per_block

    def page_spec(u):
        return pl.BlockSpec((None, None) + cache_rows.shape[2:],
                            lambda b, h, pt, ix: (layer, pt[b, pg(b, h, u, ix)], 0, 0))

    def bias_spec(u):
        return pl.BlockSpec((None, None, 1, page), lambda b, h, pt, ix: (h, pg(b, h, u, ix), 0, 0))

    grid_spec = pltpu.PrefetchScalarGridSpec(
        num_scalar_prefetch=2,
        grid=(batch, MOBA_HEADS),
        in_specs=[
            pl.BlockSpec((None, 1, HEAD_DIM), lambda b, h, pt, ix: (b, 0, NSA_HEADS + SB_HEADS + h)),
            pl.BlockSpec((None, None, 1, HEAD_DIM), lambda b, h, pt, ix: (b, h, 0, 0)),
            pl.BlockSpec((None, None, 1, HEAD_DIM), lambda b, h, pt, ix: (b, MOBA_HEADS + h, 0, 0)),
            pl.BlockSpec((None, 1, LANES), lambda b, h, pt, ix: (h, 0, 0)),
        ]
        + [page_spec(u) for u in range(n)] + [bias_spec(u) for u in range(n)],
        out_specs=pl.BlockSpec((None, 1, HEAD_DIM), lambda b, h, pt, ix: (b, 0, h)),
    )
    return pl.pallas_call(
        _moba_sample_kernel,
        grid_spec=grid_spec,
        out_shape=jax.ShapeDtypeStruct((batch, 1, MOBA_HEADS * HEAD_DIM), F32),
        compiler_params=_cparams(("arbitrary", "arbitrary")),
        name="moba_sample",
    )(page_table, idx, q3, rows4, rows4, bias0, *([cache_rows] * n), *([bias_pages] * n))


def _nsa_select_kernel(pt_ref, qg_ref, alpha_ref, pe_ref, w1_ref, w2_ref, biasc_ref, *refs):
    del pt_ref
    pages = refs[:NSA_PAGES_PER_STEP]
    ocmp_ref, idx_ref, z_s = refs[NSA_PAGES_PER_STEP:]
    s = pl.program_id(1)
    page = pages[0].shape[0]
    per_page = page // NSA_BLOCK
    rows = NSA_PAGES_PER_STEP * per_page
    nb = z_s.shape[0]

    alpha = alpha_ref[...]
    zs = [jnp.sum(ref[...].reshape((per_page, NSA_BLOCK) + ref.shape[1:]) * alpha[None], axis=1) for ref in pages]
    z_s[pl.ds(pl.multiple_of(s * rows, rows), rows)] = jnp.concatenate(zs, axis=0)

    @pl.when(s == pl.num_programs(1) - 1)
    def _():
        n_lane = _iota((1, nb), 1)
        pe_sum = jnp.sum(alpha * pe_ref[...], axis=0)
        for g in range(NSA_KV):
            kc, vc = [_compress_mlp(z_s[:, c * NSA_KV + g, :] + pe_sum[c * NSA_KV + g:c * NSA_KV + g + 1, :],
                                    w1_ref[c], w2_ref[c]).astype(BF16) for c in range(2)]
            lc = _dot_nt(qg_ref[g].astype(BF16), kc) + biasc_ref[g]
            lc = lc - jnp.max(lc, axis=-1, keepdims=True)
            e = jnp.exp(lc)
            p = e / jnp.maximum(jnp.sum(e, axis=-1, keepdims=True), 1e-30)
            ocmp_ref[g] = _dot(p.astype(BF16), vc)
            imp = jnp.sum(p[0:NSA_GROUP], axis=0, keepdims=True)
            by_lane = jnp.broadcast_to(imp, (nb, nb))
            by_row = by_lane.T
            m_idx = _iota((nb, nb), 0)
            n_idx = _iota((nb, nb), 1)
            beats = jnp.where(by_row > by_lane, 1.0,
                              jnp.where((by_row == by_lane) & (m_idx < n_idx), 1.0, 0.0))
            beats = jnp.where(m_idx >= 1, beats, 0.0)
            rank = jnp.sum(beats, axis=0, keepdims=True)
            k_idx = _iota((NSA_N_SEL, nb), 0)
            hit = ((jnp.broadcast_to(rank, (NSA_N_SEL, nb)) == k_idx.astype(F32)) & (n_lane >= 1)
                   & (k_idx < NSA_N_SEL - 2))
            pick = jnp.sum(jnp.where(hit, n_lane.astype(F32), 0.0), axis=-1, keepdims=True)
            idx_ref[g] = jnp.broadcast_to(pick, (NSA_N_SEL, LANES)).astype(I32)


def _nsa_select(page_table, qg, cache, alpha_w, pe_w, w1, w2, biasc_s, layer):
    batch, n_pages = page_table.shape
    page = cache.shape[2]
    steps = n_pages // NSA_PAGES_PER_STEP
    nb = n_pages * page // NSA_BLOCK
    cmp_block = (2 * NSA_KV, HEAD_DIM)

    def page_map(k):
        return lambda b, s, pt: (layer, pt[b, s * NSA_PAGES_PER_STEP + k], 0, 0, 0, 0)

    grid_spec = pltpu.PrefetchScalarGridSpec(
        num_scalar_prefetch=1,
        grid=(batch, steps),
        in_specs=[
            pl.BlockSpec((None, NSA_KV, SUBLANES, HEAD_DIM), lambda b, s, pt: (b, 0, 0, 0)),
            pl.BlockSpec((None, NSA_BLOCK) + cmp_block, lambda b, s, pt: (layer, 0, 0, 0)),
            pl.BlockSpec((None, NSA_BLOCK) + cmp_block, lambda b, s, pt: (layer, 0, 0, 0)),
            pl.BlockSpec((None, 2, HEAD_DIM, HEAD_DIM), lambda b, s, pt: (layer, 0, 0, 0)),
            pl.BlockSpec((None, 2, HEAD_DIM, HEAD_DIM), lambda b, s, pt: (layer, 0, 0, 0)),
            pl.BlockSpec((NSA_KV, SUBLANES, nb), lambda b, s, pt: (0, 0, 0)),
        ] + [pl.BlockSpec((None, None, page, None) + cmp_block, page_map(k)) for k in range(NSA_PAGES_PER_STEP)],
        out_specs=[
            pl.BlockSpec((None, NSA_KV, SUBLANES, HEAD_DIM), lambda b, s, pt: (b, 0, 0, 0)),
            pl.BlockSpec((None, NSA_KV, NSA_N_SEL, LANES), lambda b, s, pt: (b, 0, 0, 0)),
        ],
        scratch_shapes=[pltpu.VMEM((nb,) + cmp_block, F32)],
    )
    return pl.pallas_call(
        _nsa_select_kernel,
        grid_spec=grid_spec,
        out_shape=[jax.ShapeDtypeStruct((batch, NSA_KV, SUBLANES, HEAD_DIM), F32),
                   jax.ShapeDtypeStruct((batch, NSA_KV, NSA_N_SEL, LANES), I32)],
        compiler_params=_cparams(("arbitrary", "arbitrary")),
        name="nsa_select",
    )(page_table, qg, alpha_w, pe_w, w1, w2, biasc_s, *([cache] * NSA_PAGES_PER_STEP))


def _nsa_sample_kernel(pt_ref, idx_ref, qg_ref, ocmp_ref, gate_ref, sknew_ref, svnew_ref, wknew_ref, wvnew_ref,
                       win_ref, biasw_ref, bias0_ref, biasb_ref, *refs):
    del pt_ref
    blk_refs, o_ref = refs[:-1], refs[-1]
    b = pl.program_id(0)
    g = pl.program_id(1)
    qf = qg_ref[...]
    q = qf.astype(BF16)
    bias0 = bias0_ref[:, 0:1]

    s_own = jnp.sum(qf * sknew_ref[...], axis=-1, keepdims=True) + bias0
    s = [_dot_nt(q, _slab(ref, 2 * NSA_KV + g, NSA_BLOCK, NSA_SLABS).astype(BF16)) + biasb_ref[idx_ref[b, g, u]]
         for u, ref in enumerate(blk_refs)]
    m = s_own
    for s_ in s:
        m = jnp.maximum(m, jnp.max(s_, axis=-1, keepdims=True))
    p_own = jnp.exp(s_own - m)
    p = [jnp.exp(s_ - m) for s_ in s]
    l = p_own + sum(jnp.sum(p_, axis=-1, keepdims=True) for p_ in p)
    acc = p_own * svnew_ref[...]
    for p_, ref in zip(p, blk_refs):
        acc = acc + _dot(p_.astype(BF16), _slab(ref, 3 * NSA_KV + g, NSA_BLOCK, NSA_SLABS).astype(BF16))
    o_slc = acc / l

    w = win_ref.shape[0] // WIN_SLABS
    wk = _slab(win_ref, g, w, WIN_SLABS).astype(BF16)
    wv = _slab(win_ref, NSA_KV + g, w, WIN_SLABS).astype(BF16)
    sw = _dot_nt(q, wk) + biasw_ref[...]
    valid = _iota((1, w), 1) >= 1
    sw = jnp.where(valid, sw, NEG)
    s_new = jnp.sum(qf * wknew_ref[...], axis=-1, keepdims=True) + bias0
    mw = jnp.maximum(jnp.max(sw, axis=-1, keepdims=True), s_new)
    pw = jnp.where(valid, jnp.exp(sw - mw), 0.0)
    p_new = jnp.exp(s_new - mw)
    den = jnp.sum(pw, axis=-1, keepdims=True) + p_new
    o_win = (_dot(pw.astype(BF16), wv) + p_new * wvnew_ref[...]) / den

    gates = jnp.broadcast_to(gate_ref[...], (SUBLANES, LANES))
    lane = _iota((SUBLANES, LANES), 1)
    row = _iota((SUBLANES, LANES), 0)

    def gate(c):
        return jnp.sum(jnp.where(lane == c * NSA_GROUP + row, gates, 0.0), axis=-1, keepdims=True)

    o = gate(0) * ocmp_ref[...] + gate(1) * o_slc + gate(2) * o_win
    for jh in range(NSA_GROUP):
        o_ref[:, jh * HEAD_DIM:(jh + 1) * HEAD_DIM] = o[jh:jh + 1, :]


def _nsa_sample(page_table, idx, qg, ocmp, gates3, nsa_rows4, win_rows4, cache_blk, state_rows, bias_blocks,
                bias_w, bias0, layer, per_page):
    batch, n_pages = page_table.shape
    n_blk = NSA_N_SEL - 1
    w = state_rows.shape[2] // WIN_SLABS
    gw = NSA_GROUP * HEAD_DIM
    nb = bias_blocks.shape[1]

    def cache_spec(u):
        def index(b, g, pt, ix):
            n = ix[b, g, u]
            return (layer, pt[b, n // per_page] * per_page + n % per_page, 0, 0)
        return pl.BlockSpec((None, None) + cache_blk.shape[2:], index)

    def new_spec(slab):
        return pl.BlockSpec((None, None, 1, HEAD_DIM), lambda b, g, pt, ix: (b, slab + g, 0, 0))

    grid_spec = pltpu.PrefetchScalarGridSpec(
        num_scalar_prefetch=2,
        grid=(batch, NSA_KV),
        in_specs=[
            pl.BlockSpec((None, None, SUBLANES, HEAD_DIM), lambda b, g, pt, ix: (b, g, 0, 0)),
            pl.BlockSpec((None, None, SUBLANES, HEAD_DIM), lambda b, g, pt, ix: (b, g, 0, 0)),
            pl.BlockSpec((None, 1, LANES), lambda b, g, pt, ix: (b, 0, g)),
            new_spec(2 * NSA_KV), new_spec(3 * NSA_KV),
            new_spec(0), new_spec(NSA_KV),
            pl.BlockSpec((None, None) + state_rows.shape[2:], lambda b, g, pt, ix: (layer, b, 0, 0)),
            pl.BlockSpec((None, SUBLANES, w), lambda b, g, pt, ix: (g, 0, 0)),
            pl.BlockSpec((None, SUBLANES, LANES), lambda b, g, pt, ix: (g, 0, 0)),
            pl.BlockSpec((None, nb, SUBLANES, NSA_BLOCK), lambda b, g, pt, ix: (g, 0, 0, 0)),
        ]
        + [cache_spec(u) for u in range(n_blk)],
        out_specs=pl.BlockSpec((None, 1, gw), lambda b, g, pt, ix: (b, 0, g)),
    )
    return pl.pallas_call(
        _nsa_sample_kernel,
        grid_spec=grid_spec,
        out_shape=jax.ShapeDtypeStruct((batch, 1, NSA_HEADS * HEAD_DIM), F32),
        compiler_params=_cparams(("arbitrary", "arbitrary")),
        name="nsa_sample",
    )(page_table, idx, qg, ocmp, gates3, nsa_rows4, nsa_rows4, win_rows4, win_rows4, state_rows,
      bias_w, bias0, bias_blocks, *([cache_blk] * n_blk))


def _rel_bucket(dist):
    exact = REL_BUCKETS // 2
    d = jnp.maximum(dist, 0)
    df = jnp.maximum(d, 1).astype(F32)
    far = exact + (jnp.log(df / exact) / math.log(REL_MAX_DIST / exact) * (REL_BUCKETS - exact)).astype(I32)
    return jnp.where(d < exact, d, jnp.minimum(far, REL_BUCKETS - 1))


def _pad_rows(a, axis, n):
    pad = [(0, 0)] * a.ndim
    pad[axis] = (0, n - a.shape[axis])
    return jnp.pad(a, pad)


def kernel(x_prompt, x_sample, cache_nsa, cache_sb, cache_moba, state_nsa_win, page_table, norm_mix, w_in,
           nsa_cmp_pe, nsa_cmp_alpha, nsa_cmp_w1, nsa_cmp_w2, rel_bias, w_out, norm_ffn, w_up, w_down,
           norm_final):
    batch, seq, d = x_prompt.shape
    dec_batch, dec_seq, _ = x_sample.shape
    depth = w_in.shape[0]
    n_pool, page = cache_nsa.shape[1], cache_nsa.shape[2]
    n_pages = page_table.shape[1]
    past = n_pages * page
    w_buf = state_nsa_win.shape[2]
    assert dec_seq == 1 and seq % ATT_TILE == 0 and ATT_TILE == MOBA_BLOCK
    assert page % NSA_BLOCK == 0 and MOBA_BLOCK % page == 0 and w_buf == NSA_WINDOW
    assert past // NSA_BLOCK >= NSA_N_SEL and past // MOBA_BLOCK >= MOBA_TOPK
    assert dec_batch <= SAMPLE_ROWS and (batch * seq) % ROW_TILE == 0
    assert NSA_SLABS == SUBLANES and SB_SLABS == SUBLANES and MOBA_SLABS == SUBLANES

    o_gate = MIX_WIDTH + KV_COLS
    w_main = w_in.astype(BF16)
    wg = w_in[:, :, o_gate:].reshape(depth, d, 3, NSA_KV, NSA_GROUP).transpose(0, 1, 3, 2, 4)
    wg = wg.reshape(depth, d, NSA_KV, 3 * NSA_GROUP)
    w_gate = _pad_rows(wg, 3, LANES).reshape(depth, d, NSA_KV * LANES).astype(BF16)
    w_out_b = w_out.astype(BF16)
    w_up_b = w_up.astype(BF16)
    w_down_b = w_down.astype(BF16)
    g_mix = norm_mix.reshape(depth, 1, d)
    g_ffn = norm_ffn.reshape(depth, 1, d)
    g_fin = norm_final.reshape(1, d)
    alpha_b = jnp.broadcast_to(nsa_cmp_alpha[..., None], nsa_cmp_alpha.shape + (HEAD_DIM,))
    cmp_w2t = nsa_cmp_w2.transpose(0, 1, 3, 2)

    def slab_params(p):
        return jnp.repeat(p, NSA_KV, axis=1).transpose(0, 2, 1, 3)

    alpha_w = slab_params(alpha_b)
    pe_w = slab_params(nsa_cmp_pe)

    max_d = max(seq, past)
    bucket_1d = _rel_bucket(jnp.arange(max_d + 1, dtype=I32))
    thr = jnp.sum(bucket_1d[None, :] < jnp.arange(REL_BUCKETS, dtype=I32)[:, None], axis=1).astype(I32)
    tbl_t = rel_bias.T
    bias_tiles = _bias_tiles(thr, tbl_t, seq // ATT_TILE)
    biasc_p = _bias_cmp(thr, tbl_t[:NSA_HEADS], seq // NSA_BLOCK, seq)

    nb_s = past // NSA_BLOCK

    def heads8(a):
        a = a.reshape((NSA_KV, NSA_GROUP) + a.shape[1:])
        return _pad_rows(a, 1, SUBLANES)

    bias_back = rel_bias[_rel_bucket(past - jnp.arange(past, dtype=I32))].T
    nsa_back = bias_back[:NSA_HEADS]
    moba_back = bias_back[NSA_HEADS:]
    biasc_s = heads8(nsa_back[:, NSA_BLOCK - 1::NSA_BLOCK])
    bias_blocks = heads8(nsa_back.reshape(NSA_HEADS, nb_s, NSA_BLOCK)).transpose(0, 2, 1, 3)
    bias_w = heads8(rel_bias[_rel_bucket(NSA_WINDOW - jnp.arange(w_buf, dtype=I32)), :NSA_HEADS].T)
    bias0 = rel_bias[_rel_bucket(jnp.zeros((), I32))]
    bias0_nsa = jnp.broadcast_to(heads8(bias0[:NSA_HEADS])[..., None], (NSA_KV, SUBLANES, LANES))
    bias_pages = moba_back.reshape(MOBA_HEADS, n_pages, 1, page)
    bias0_moba = jnp.broadcast_to(bias0[NSA_HEADS:][:, None, None], (MOBA_HEADS, 1, LANES))

    per_page = page // NSA_BLOCK
    cache_nsa_half = cache_nsa.reshape(depth, n_pool, page, 2, NSA_SLABS // 2, HEAD_DIM)
    cache_nsa_blk =cache_nsa.reshape(depth, n_pool * per_page, NSA_BLOCK * NSA_SLABS, HEAD_DIM)
    cache_sb_rows = cache_sb.reshape(depth, n_pool, page * SB_SLABS, HEAD_DIM)
    cache_moba_rows = cache_moba.reshape(depth, n_pool, page * MOBA_SLABS, HEAD_DIM)
    state_rows = state_nsa_win.reshape(depth, dec_batch, w_buf * WIN_SLABS, HEAD_DIM)

    xp = x_prompt.reshape(batch * seq, d)
    xs = _pad_rows(x_sample.reshape(dec_batch, d), 0, SAMPLE_ROWS)
    win_keep = min(NSA_WINDOW, seq)
    kv_p = ()
    kv_s = ()
    win_s_out = []

    for l in range(depth):
        last = l == depth - 1
        q, *kv_p, gates, kvb = _in_proj(xp, g_mix, w_main, w_gate, l, depth, IN_ROW_TILE, kv_p)
        mix_nsa = _nsa_prompt(q, kvb, gates, alpha_b, nsa_cmp_pe, nsa_cmp_w1, nsa_cmp_w2, cmp_w2t,
                              biasc_p, bias_tiles, l, batch, seq)
        mix_sb = _sb_prompt(q, kvb, batch, seq)
        mix_moba = _moba_prompt(q, kvb, bias_tiles, batch, seq)
        xp = _out_proj(xp, mix_nsa, mix_sb, mix_moba, w_out_b, l, ROW_TILE)
        xp = _ffn(xp, g_ffn, w_up_b, w_down_b, g_fin, l, ROW_TILE, last)

        q_s, *kv_s, gates_s, _ = _in_proj(xs, g_mix, w_main, w_gate, l, depth, SAMPLE_ROWS, kv_s)
        nsa_s, win_s, _, moba_s = [a.reshape(depth, -1, HEAD_DIM)[l] for a in kv_s]
        q3 = q_s.astype(F32).reshape(SAMPLE_ROWS, 1, MIX_WIDTH)
        qh = q3[:dec_batch, 0].reshape(dec_batch, N_HEADS, HEAD_DIM)
        qg = _pad_rows(qh[:, :NSA_HEADS].reshape(dec_batch, NSA_KV, NSA_GROUP, HEAD_DIM), 2, SUBLANES)
        q_moba8 = _pad_rows(qh[:, NSA_HEADS + SB_HEADS:], 1, MOBA_SLABS)
        nsa_s4 = nsa_s.reshape(SAMPLE_ROWS, NSA_SLABS, 1, HEAD_DIM)
        win_s4 = win_s.reshape(SAMPLE_ROWS, WIN_SLABS, 1, HEAD_DIM)
        moba_s4 = moba_s.reshape(SAMPLE_ROWS, MOBA_SLABS, 1, HEAD_DIM)
        gates_s3 = gates_s.reshape(SAMPLE_ROWS, 1, 2 * LANES)

        ocmp, sel = _nsa_select(page_table, qg, cache_nsa_half, alpha_w, pe_w, nsa_cmp_w1, nsa_cmp_w2, biasc_s, l)
        o_nsa = _nsa_sample(page_table, sel[:, :, :, 0], qg, ocmp, gates_s3, nsa_s4, win_s4, cache_nsa_blk,
                            state_rows, bias_blocks, bias_w, bias0_nsa, l, per_page)
        o_sb = _sb_sample(page_table, q3, cache_sb_rows, l)
        top = _moba_select(page_table, q_moba8, cache_moba, l)
        o_moba = _moba_sample(page_table, top[:, :, :MOBA_HEADS, 0], q3, cache_moba_rows, moba_s4,
                              bias_pages, bias0_moba, l)

        def rows16(o):
            return _pad_rows(o.reshape(dec_batch, -1), 0, SAMPLE_ROWS).astype(BF16)

        xs = _out_proj(xs, rows16(o_nsa), rows16(o_sb), rows16(o_moba), w_out_b, l, SAMPLE_ROWS)
        xs = _ffn(xs, g_ffn, w_up_b, w_down_b, g_fin, l, SAMPLE_ROWS, last)
        new_win = win_s[:dec_batch * WIN_SLABS].reshape(dec_batch, 1, 2, NSA_KV, HEAD_DIM)
        win_s_out.append(jnp.concatenate([state_nsa_win[l][:, 1:], new_win], axis=1))

    def rows_p(a, *dims):
        return a.reshape((depth, batch, seq) + dims + (HEAD_DIM,))

    def rows_s(a, *dims):
        return a.reshape((depth, SAMPLE_ROWS, 1) + dims + (HEAD_DIM,))[:, :dec_batch]

    nsa_p, win_p, sb_p, moba_p = kv_p
    nsa_s, win_s, sb_s, moba_s = kv_s
    y_prompt = xp.reshape(batch, seq, d)
    y_sample = xs[:dec_batch].reshape(dec_batch, 1, d)
    return (y_prompt, y_sample,
            rows_p(nsa_p, 4, NSA_KV), rows_s(nsa_s, 4, NSA_KV),
            rows_p(sb_p, 2, SB_HEADS), rows_s(sb_s, 2, SB_HEADS),
            rows_p(moba_p, 2, MOBA_HEADS), rows_s(moba_s, 2, MOBA_HEADS),
            rows_p(win_p, 2, NSA_KV)[:, :, seq - win_keep:], jnp.stack(win_s_out))
```

```python
import functools
import math

import jax
import jax.numpy as jnp
from jax import lax
from jax.experimental import pallas as pl
from jax.experimental.pallas import tpu as pltpu

F32 = jnp.float32
BF16 = jnp.bfloat16
I32 = jnp.int32

HEAD_DIM = 128
NSA_HEADS = 8
SB_HEADS = 4
MOBA_HEADS = 4
NSA_GROUP = 4
NSA_KV = 2
N_HEADS = NSA_HEADS + SB_HEADS + MOBA_HEADS
MIX_WIDTH = N_HEADS * HEAD_DIM
NSA_SLABS = 4 * NSA_KV
WIN_SLABS = 2 * NSA_KV
SB_SLABS = 2 * SB_HEADS
MOBA_SLABS = 2 * MOBA_HEADS
NSA_ROW_COLS = NSA_SLABS * HEAD_DIM
WIN_ROW_COLS = WIN_SLABS * HEAD_DIM
SB_ROW_COLS = SB_SLABS * HEAD_DIM
MOBA_ROW_COLS = MOBA_SLABS * HEAD_DIM
KV_COLS = NSA_ROW_COLS + WIN_ROW_COLS + SB_ROW_COLS + MOBA_ROW_COLS
NSA_BLOCK = 64
NSA_N_SEL = 16
NSA_WINDOW = 512
MOBA_BLOCK = 256
MOBA_TOPK = 3
REL_BUCKETS = 32
REL_MAX_DIST = 2048
RMS_EPS = 1e-6
ATTN_SCALE = HEAD_DIM ** -0.5
NEG = -1e30
FORCED = 1e4

V7X_VMEM_LIMIT_BYTES = 56 * 1024 * 1024
LANES = 128
SUBLANES = 8
COL_TILE = 512
ROW_TILE = 1024
IN_ROW_TILE = 1024
ATT_TILE = 256
SAMPLE_ROWS = 16
SLABS_PER_TILE = COL_TILE // HEAD_DIM


def _cparams(sem):
    return pltpu.CompilerParams(dimension_semantics=sem, vmem_limit_bytes=V7X_VMEM_LIMIT_BYTES)


def _dot(a, b):
    return jnp.dot(a, b, preferred_element_type=F32)


def _dot_nt(a, b):
    return lax.dot_general(a, b, (((1,), (1,)), ((), ())), preferred_element_type=F32)


def _split_bf16(a, terms):
    parts = []
    for _ in range(terms):
        p = a.astype(BF16)
        parts.append(p)
        a = a - p.astype(F32)
    return parts


def _dot_split(a, b_bf16, terms=2):
    return sum(_dot(p, b_bf16) for p in _split_bf16(a, terms))


def _iota(shape, dim):
    return lax.broadcasted_iota(I32, shape, dim)


def _rms(x, g):
    return (x * lax.rsqrt(jnp.mean(x * x, axis=-1, keepdims=True) + RMS_EPS)) * g


def _softplus(z):
    return jnp.maximum(z, 0.0) + jnp.log(1.0 + jnp.exp(-jnp.abs(z)))


def _head(a, h):
    return a[:, h * HEAD_DIM:(h + 1) * HEAD_DIM]


def _slab(ref, s, n, slabs):
    return ref[pl.ds(s, n, stride=slabs), :]


N_Q_TILES = MIX_WIDTH // COL_TILE
N_KV_TILES = KV_COLS // COL_TILE
KV_GROUPS = (("nsa", NSA_SLABS, 0), ("win", WIN_SLABS, 2), ("sb", SB_SLABS, 3), ("moba", MOBA_SLABS, 5))


def _in_proj_kernel(x_ref, g_ref, w_ref, wg_ref, *refs):
    q_ref, nsa_ref, win_ref, sb_ref, moba_ref, gate_ref, kvb_ref, h_ref = refs[-8:]
    j = pl.program_id(1)
    tm = x_ref.shape[0]

    @pl.when(j == 0)
    def _():
        h = _rms(x_ref[...], g_ref[...]).astype(BF16)
        h_ref[...] = h
        gate_ref[...] = jax.nn.sigmoid(_dot(h, wg_ref[...]))

    acc = _dot(h_ref[...], w_ref[...])

    @pl.when(j < N_Q_TILES)
    def _():
        q_ref[...] = (acc * ATTN_SCALE).astype(BF16)

    @pl.when(j >= N_Q_TILES)
    def _():
        kvb_ref[...] = acc.astype(BF16)

    for (_, slabs, start), ref in zip(KV_GROUPS, (nsa_ref, win_ref, sb_ref, moba_ref)):
        for tt in range(slabs // SLABS_PER_TILE):
            @pl.when(j == N_Q_TILES + start + tt)
            def _(ref=ref, slabs=slabs, tt=tt):
                for s in range(SLABS_PER_TILE):
                    ref[pl.ds(tt * SLABS_PER_TILE + s, tm, stride=slabs), :] = _head(acc, s)


def _in_proj(x, g, w_main, w_gate, layer, depth, tm, prev_kv):
    m, d = x.shape
    n_tiles = N_Q_TILES + N_KV_TILES
    row_tiles = m // tm

    def clamp_map(lo, count):
        return lambda i, j: (i, jnp.clip(j - lo, 0, count - 1))

    out_shape = [jax.ShapeDtypeStruct((m, MIX_WIDTH), BF16)]
    out_specs = [pl.BlockSpec((tm, COL_TILE), clamp_map(0, N_Q_TILES))]
    for _, slabs, _ in KV_GROUPS:
        out_shape.append(jax.ShapeDtypeStruct((depth * m * slabs, HEAD_DIM), F32))
        out_specs.append(pl.BlockSpec((tm * slabs, HEAD_DIM), lambda i, j: (layer * row_tiles + i, 0),
                                      pipeline_mode=pl.Buffered(1)))
    out_shape += [jax.ShapeDtypeStruct((m, 2 * LANES), F32), jax.ShapeDtypeStruct((m, KV_COLS), BF16)]
    out_specs += [pl.BlockSpec((tm, 2 * LANES), lambda i, j: (i, 0)),
                  pl.BlockSpec((tm, COL_TILE), clamp_map(N_Q_TILES, N_KV_TILES))]
    n_in = 4
    return pl.pallas_call(
        _in_proj_kernel,
        grid=(row_tiles, n_tiles),
        in_specs=[
            pl.BlockSpec((tm, d), lambda i, j: (i, 0)),
            pl.BlockSpec((None, 1, d), lambda i, j: (layer, 0, 0)),
            pl.BlockSpec((None, d, COL_TILE), lambda i, j: (layer, 0, j)),
            pl.BlockSpec((None, d, 2 * LANES), lambda i, j: (layer, 0, 0)),
        ] + [pl.BlockSpec(memory_space=pl.ANY)] * len(prev_kv),
        out_specs=out_specs,
        out_shape=out_shape,
        input_output_aliases={n_in + k: 1 + k for k in range(len(prev_kv))},
        scratch_shapes=[pltpu.VMEM((tm, d), BF16)],
        compiler_params=_cparams(("arbitrary", "arbitrary")),
        name="in_proj",
    )(x, g, w_main, w_gate, *prev_kv)


def _out_proj_kernel(x_ref, a_ref, b_ref, c_ref, w_ref, o_ref):
    n_a = a_ref.shape[1]
    n_b = b_ref.shape[1]
    acc = _dot(a_ref[...], w_ref[0:n_a, :])
    acc += _dot(b_ref[...], w_ref[n_a:n_a + n_b, :])
    acc += _dot(c_ref[...], w_ref[n_a + n_b:, :])
    o_ref[...] = x_ref[...] + acc


def _out_proj(x, mix_nsa, mix_sb, mix_moba, w_out, layer, tm):
    m, d = x.shape
    return pl.pallas_call(
        _out_proj_kernel,
        grid=(m // tm, d // COL_TILE),
        in_specs=[
            pl.BlockSpec((tm, COL_TILE), lambda i, j: (i, j)),
            pl.BlockSpec((tm, mix_nsa.shape[1]), lambda i, j: (i, 0)),
            pl.BlockSpec((tm, mix_sb.shape[1]), lambda i, j: (i, 0)),
            pl.BlockSpec((tm, mix_moba.shape[1]), lambda i, j: (i, 0)),
            pl.BlockSpec((None, MIX_WIDTH, COL_TILE), lambda i, j: (layer, 0, j)),
        ],
        out_specs=pl.BlockSpec((tm, COL_TILE), lambda i, j: (i, j)),
        out_shape=jax.ShapeDtypeStruct((m, d), F32),
        compiler_params=_cparams(("arbitrary", "arbitrary")),
        name="out_proj",
    )(x, mix_nsa, mix_sb, mix_moba, w_out)


def _ffn_kernel(x_ref, g_ref, wu_ref, wd_ref, gf_ref, o_ref, h_ref, *, final_norm):
    j = pl.program_id(1)

    @pl.when(j == 0)
    def _():
        x = x_ref[...]
        h_ref[...] = _rms(x, g_ref[...]).astype(BF16)
        o_ref[...] = x

    a = jnp.maximum(_dot(h_ref[...], wu_ref[...]), 0.0)
    o_ref[...] += _dot((a * a).astype(BF16), wd_ref[...])

    if final_norm:
        @pl.when(j == pl.num_programs(1) - 1)
        def _():
            o_ref[...] = _rms(o_ref[...], gf_ref[...])


def _ffn(x, g, w_up, w_down, g_final, layer, tm, final_norm):
    m, d = x.shape
    d_ff = w_up.shape[2]
    return pl.pallas_call(
        functools.partial(_ffn_kernel, final_norm=final_norm),
        grid=(m // tm, d_ff // COL_TILE),
        in_specs=[
            pl.BlockSpec((tm, d), lambda i, j: (i, 0)),
            pl.BlockSpec((None, 1, d), lambda i, j: (layer, 0, 0)),
            pl.BlockSpec((None, d, COL_TILE), lambda i, j: (layer, 0, j)),
            pl.BlockSpec((None, COL_TILE, d), lambda i, j: (layer, j, 0)),
            pl.BlockSpec((1, d), lambda i, j: (0, 0)),
        ],
        out_specs=pl.BlockSpec((tm, d), lambda i, j: (i, 0)),
        out_shape=jax.ShapeDtypeStruct((m, d), F32),
        scratch_shapes=[pltpu.VMEM((tm, d), BF16)],
        compiler_params=_cparams(("arbitrary", "arbitrary")),
        name="ffn",
    )(x, g, w_up, w_down, g_final)


def _bias_grid_kernel(thr_ref, tbl_ref, o_ref, *, offset, step, row_step):
    h = pl.program_id(0)
    shape = o_ref.shape
    dist = offset + step * pl.program_id(1) + row_step * _iota(shape, 0) + _iota(shape, 1)
    acc = jnp.full(shape, tbl_ref[h, 0], F32)
    for k in range(1, REL_BUCKETS):
        acc = jnp.where(dist >= thr_ref[k], tbl_ref[h, k], acc)
    o_ref[...] = acc


def _bias_tiles(thr, tbl_t, nd):
    n_heads = tbl_t.shape[0]
    grid_spec = pltpu.PrefetchScalarGridSpec(
        num_scalar_prefetch=2,
        grid=(n_heads, nd),
        in_specs=[],
        out_specs=pl.BlockSpec((None, None, ATT_TILE, ATT_TILE), lambda h, d, thr, tbl: (h, d, 0, 0)),
    )
    return pl.pallas_call(
        functools.partial(_bias_grid_kernel, offset=0, step=ATT_TILE, row_step=-1),
        grid_spec=grid_spec,
        out_shape=jax.ShapeDtypeStruct((n_heads, nd, ATT_TILE, ATT_TILE), F32),
        compiler_params=_cparams(("arbitrary", "arbitrary")),
        name="bias_tiles",
    )(thr, tbl_t)


def _bias_cmp(thr, tbl_t, nb, seq):
    n_heads = tbl_t.shape[0]
    grid_spec = pltpu.PrefetchScalarGridSpec(
        num_scalar_prefetch=2,
        grid=(n_heads, seq // ATT_TILE),
        in_specs=[],
        out_specs=pl.BlockSpec((None, nb, ATT_TILE), lambda h, i, thr, tbl: (h, 0, i)),
    )
    return pl.pallas_call(
        functools.partial(_bias_grid_kernel, offset=-(NSA_BLOCK - 1), step=ATT_TILE, row_step=-NSA_BLOCK),
        grid_spec=grid_spec,
        out_shape=jax.ShapeDtypeStruct((n_heads, nb, seq), F32),
        compiler_params=_cparams(("arbitrary", "arbitrary")),
        name="bias_cmp",
    )(thr, tbl_t)


def _online_steps(m_s, l_s, acc_s, items):
    weights = []
    for slot, s, mask, _ in items:
        if mask is not None:
            s = jnp.where(mask, s, NEG)
        m_prev = m_s[slot]
        m_new = jnp.maximum(m_prev, jnp.max(s, axis=0, keepdims=True))
        alpha = jnp.exp(m_prev - m_new)
        p = jnp.exp(s - m_new)
        l_s[slot] = alpha * l_s[slot] + jnp.sum(p, axis=0, keepdims=True)
        m_s[slot] = m_new
        weights.append((alpha, p.astype(BF16)))
    for (slot, _, _, vt), (alpha, p) in zip(items, weights):
        acc_s[slot] = alpha * acc_s[slot] + _dot(vt, p)


def _reset_state(m_s, l_s, acc_s):
    m_s[...] = jnp.full(m_s.shape, NEG, F32)
    l_s[...] = jnp.zeros(l_s.shape, F32)
    acc_s[...] = jnp.zeros(acc_s.shape, F32)


def _tile(ref, j, t):
    return ref[pl.ds(pl.multiple_of(j * t, t), t), :]


def _tile_t(ref, j, t):
    return ref[:, pl.ds(pl.multiple_of(j * t, t), t)]


def _transpose_into(dst_ref, src_ref, heads):
    seq = src_ref.shape[0]
    for j in range(seq // ATT_TILE):
        rows = slice(j * ATT_TILE, (j + 1) * ATT_TILE)
        for h in range(heads):
            cols = slice(h * HEAD_DIM, (h + 1) * HEAD_DIM)
            dst_ref[cols, rows] = src_ref[rows, cols].astype(F32).T.astype(BF16)


def _rank_rows(score, n):
    idx = _iota((n, 1), 0)
    rank = jnp.zeros(score.shape, F32)
    for mm in range(n):
        sm = score[mm:mm + 1, :]
        first = (mm < idx).astype(F32)
        rank += jnp.where(sm > score, 1.0, jnp.where(sm == score, first, 0.0))
    return rank


def _store_heads(o_ref, outs_t):
    for h, o_t in enumerate(outs_t):
        o_ref[:, h * HEAD_DIM:(h + 1) * HEAD_DIM] = o_t.T.astype(o_ref.dtype)


def _sb_prompt_kernel(q_ref, k_ref, v_ref, o_ref, vt_s, carry_s, acc_s):
    t = ATT_TILE
    i = pl.program_id(1)

    @pl.when(i == 0)
    def _():
        _transpose_into(vt_s, v_ref, SB_HEADS)

    key = _iota((t, t), 0)
    qry = _iota((t, t), 1)
    u_suffix = (qry >= key).astype(BF16)
    strict = key < qry
    carry_s[...] = jnp.zeros(carry_s.shape, F32)
    acc_s[...] = jnp.zeros(acc_s.shape, F32)

    def step(j, mask):
        kt = _tile(k_ref, j, t)
        vt = _tile_t(vt_s, j, t)
        heads = range(SB_HEADS)
        z = [_dot_nt(_head(kt, h), _head(q_ref, h)) for h in heads]
        sp = [_softplus(z[h]) for h in heads]
        lk = [-sp[h] if mask is None else jnp.where(mask, -sp[h], 0.0) for h in heads]
        parts = [_split_bf16(lk[h], 2) for h in heads]
        incl = [sum(_dot(u_suffix, part) for part in parts[h]) for h in heads]
        weights = []
        for h in heads:
            carry = carry_s[h]
            a = jnp.exp(z[h] - sp[h] + (carry + incl[h] - lk[h]))
            if mask is not None:
                a = jnp.where(mask, a, 0.0)
            weights.append(a.astype(BF16))
            carry_s[h] = carry + incl[h][0:1, :]
        for h in heads:
            acc_s[h] += _dot(vt[h * HEAD_DIM:(h + 1) * HEAD_DIM, :], weights[h])

    step(i, strict)

    def body(d, _):
        step(i - d, None)
        return 0

    lax.fori_loop(1, i + 1, body, 0)
    _store_heads(o_ref, [acc_s[h] for h in range(SB_HEADS)])


def _sb_prompt(q, kvb, batch, seq):
    nq = seq // ATT_TILE
    hw = SB_HEADS * HEAD_DIM
    kv0 = (NSA_ROW_COLS + WIN_ROW_COLS) // hw
    q0 = NSA_HEADS * HEAD_DIM // hw
    return pl.pallas_call(
        _sb_prompt_kernel,
        grid=(batch, nq),
        in_specs=[
            pl.BlockSpec((ATT_TILE, hw), lambda b, i: (b * nq + i, q0)),
            pl.BlockSpec((seq, hw), lambda b, i: (b, kv0)),
            pl.BlockSpec((seq, hw), lambda b, i: (b, kv0 + 1)),
        ],
        out_specs=pl.BlockSpec((ATT_TILE, hw), lambda b, i: (b * nq + i, 0)),
        out_shape=jax.ShapeDtypeStruct((batch * seq, hw), BF16),
        scratch_shapes=[pltpu.VMEM((hw, seq), BF16),
                        pltpu.VMEM((SB_HEADS, 1, ATT_TILE), F32),
                        pltpu.VMEM((SB_HEADS, HEAD_DIM, ATT_TILE), F32)],
        compiler_params=_cparams(("arbitrary", "arbitrary")),
        name="sb_prompt",
    )(q, kvb, kvb)


def _moba_prompt_kernel(q_ref, k_ref, v_ref, bias_ref, o_ref, vt_s, kmean_s, sel_s, m_s, l_s, acc_s):
    t = ATT_TILE
    i = pl.program_id(1)
    seq = k_ref.shape[0]
    nb = seq // t

    @pl.when(i == 0)
    def _():
        _transpose_into(vt_s, v_ref, MOBA_HEADS)
        avg = jnp.where(_iota((nb, seq), 1) // t == _iota((nb, seq), 0), 1.0 / t, 0.0).astype(BF16)
        kmean_s[...] = _dot(avg, k_ref[...])

    nrow = _iota((nb, 1), 0)
    past = nrow < i
    for h in range(MOBA_HEADS):
        q = _head(q_ref, h)
        gs = sum(_dot_nt(part, q) for part in _split_bf16(_head(kmean_s, h), 2))
        rank = _rank_rows(jnp.where(past, gs, NEG), nb)
        sel_s[h] = jnp.where(past & (rank < MOBA_TOPK), 1.0, 0.0)

    _reset_state(m_s, l_s, acc_s)
    causal = _iota((t, t), 0) <= _iota((t, t), 1)

    def step(d, diagonal):
        j = i - d
        kt = _tile(k_ref, j, t)
        vt = _tile_t(vt_s, j, t)
        items = []
        for h in range(MOBA_HEADS):
            mask = causal if diagonal else sel_s[h, pl.ds(j, 1), :] > 0.5
            s = _dot_nt(_head(kt, h), _head(q_ref, h)) + bias_ref[h, d]
            items.append((h, s, mask, vt[h * HEAD_DIM:(h + 1) * HEAD_DIM, :]))
        _online_steps(m_s, l_s, acc_s, items)

    step(0, True)

    def body(d, _):
        step(d, False)
        return 0

    lax.fori_loop(1, i + 1, body, 0)
    _store_heads(o_ref, [acc_s[h] / l_s[h] for h in range(MOBA_HEADS)])


def _moba_prompt(q, kvb, bias_tiles, batch, seq):
    nq = seq // ATT_TILE
    hw = MOBA_HEADS * HEAD_DIM
    kv0 = (NSA_ROW_COLS + WIN_ROW_COLS + SB_ROW_COLS) // hw
    q0 = (NSA_HEADS + SB_HEADS) * HEAD_DIM // hw
    nd = bias_tiles.shape[1]
    nb = seq // MOBA_BLOCK
    return pl.pallas_call(
        _moba_prompt_kernel,
        grid=(batch, nq),
        in_specs=[
            pl.BlockSpec((ATT_TILE, hw), lambda b, i: (b * nq + i, q0)),
            pl.BlockSpec((seq, hw), lambda b, i: (b, kv0)),
            pl.BlockSpec((seq, hw), lambda b, i: (b, kv0 + 1)),
            pl.BlockSpec((MOBA_HEADS, nd, ATT_TILE, ATT_TILE), lambda b, i: (NSA_HEADS // MOBA_HEADS, 0, 0, 0)),
        ],
        out_specs=pl.BlockSpec((ATT_TILE, hw), lambda b, i: (b * nq + i, 0)),
        out_shape=jax.ShapeDtypeStruct((batch * seq, hw), BF16),
        scratch_shapes=[
            pltpu.VMEM((hw, seq), BF16),
            pltpu.VMEM((nb, hw), F32),
            pltpu.VMEM((MOBA_HEADS, nb, ATT_TILE), F32),
            pltpu.VMEM((MOBA_HEADS, 1, ATT_TILE), F32),
            pltpu.VMEM((MOBA_HEADS, 1, ATT_TILE), F32),
            pltpu.VMEM((MOBA_HEADS, HEAD_DIM, ATT_TILE), F32),
        ],
        compiler_params=_cparams(("arbitrary", "arbitrary")),
        name="moba_prompt",
    )(q, kvb, kvb, bias_tiles)


def _compress_hidden(z, w1):
    hid = _dot(z.astype(BF16), w1.astype(BF16))
    return (hid * jax.nn.sigmoid(hid)).astype(BF16)


def _nsa_prompt_kernel(q_ref, ck_ref, cv_ref, sk_ref, sv_ref, wk_ref, wv_ref, alpha_ref, pe_ref,
                       w1_ref, w2_ref, w2t_ref, biasc_ref, bias_ref, gate_ref, o_ref,
                       kc_s, vct_s, svt_s, wvt_s, ocmp_s, m_s, l_s, acc_s):
    t = ATT_TILE
    i = pl.program_id(2)
    seq = ck_ref.shape[0]
    nb = seq // NSA_BLOCK
    per_tile = t // NSA_BLOCK

    @pl.when(i == 0)
    def _():
        _transpose_into(svt_s, sv_ref, 1)
        _transpose_into(wvt_s, wv_ref, 1)
        hid = []
        for c, src in enumerate((ck_ref, cv_ref)):
            x = src[...].astype(F32).reshape(nb, NSA_BLOCK, HEAD_DIM)
            z = jnp.sum((x + pe_ref[c][None]) * alpha_ref[c][None], axis=1)
            hid.append(_compress_hidden(z, w1_ref[c]))
        kc_s[...] = _dot(hid[0], w2_ref[0].astype(BF16)).astype(BF16)
        vct_s[...] = _dot_nt(w2t_ref[1].astype(BF16), hid[1]).astype(BF16)

    tpos = i * t + _iota((1, t), 1)
    nrow = _iota((nb, 1), 0)
    complete = tpos >= nrow * NSA_BLOCK + (NSA_BLOCK - 1)
    kc = kc_s[...]
    vct = vct_s[...]
    imp = jnp.zeros((nb, t), F32)
    for jh in range(NSA_GROUP):
        z = jnp.where(complete, _dot_nt(kc, _head(q_ref, jh)) + biasc_ref[jh], NEG)
        z = z - jnp.max(z, axis=0, keepdims=True)
        e = jnp.where(complete, jnp.exp(z), 0.0)
        p = e / jnp.maximum(jnp.sum(e, axis=0, keepdims=True), 1e-30)
        imp += p
        ocmp_s[jh] = _dot(vct, p.astype(BF16))

    forced = (nrow == tpos // NSA_BLOCK) | (nrow == 0)
    score = jnp.where(forced, FORCED, jnp.where(complete, imp, NEG))
    rank = _rank_rows(score, nb)
    sel = jnp.where((rank < NSA_N_SEL) & (score > 0.5 * NEG), 1.0, 0.0).astype(BF16)

    def key_mask(j):
        expand = (_iota((t, nb), 1) == j * per_tile + _iota((t, nb), 0) // NSA_BLOCK).astype(BF16)
        return _dot(expand, sel) > 0.5

    _reset_state(m_s, l_s, acc_s)
    key = _iota((t, t), 0)
    qry = _iota((t, t), 1)

    def tile_step(d, with_window):
        j = i - d
        sk = _tile(sk_ref, j, t)
        svt = _tile_t(svt_s, j, t)
        slc_mask = key_mask(j)
        if with_window:
            wk = _tile(wk_ref, j, t)
            wvt = _tile_t(wvt_s, j, t)
            dist = d * t + qry - key
            win_mask = (dist >= 0) & (dist < NSA_WINDOW)
            slc_mask = slc_mask & (dist >= 0)
        items = []
        for jh in range(NSA_GROUP):
            qh = _head(q_ref, jh)
            bias = bias_ref[jh, d]
            items.append((jh, _dot_nt(sk, qh) + bias, slc_mask, svt))
            if with_window:
                items.append((NSA_GROUP + jh, _dot_nt(wk, qh) + bias, win_mask, wvt))
        _online_steps(m_s, l_s, acc_s, items)

    tile_step(0, True)
    n_win = NSA_WINDOW // t

    def body_win(d, _):
        tile_step(d, True)
        return 0

    def body_far(d, _):
        tile_step(d, False)
        return 0

    lax.fori_loop(1, jnp.minimum(i, n_win) + 1, body_win, 0)
    lax.fori_loop(n_win + 1, i + 1, body_far, 0)

    gates_t = gate_ref[...].T
    outs = []
    for jh in range(NSA_GROUP):
        g_cmp = gates_t[jh:jh + 1, :]
        g_slc = gates_t[NSA_GROUP + jh:NSA_GROUP + jh + 1, :]
        g_win = gates_t[2 * NSA_GROUP + jh:2 * NSA_GROUP + jh + 1, :]
        outs.append(g_cmp * ocmp_s[jh] + g_slc * (acc_s[jh] / l_s[jh])
                    + g_win * (acc_s[NSA_GROUP + jh] / l_s[NSA_GROUP + jh]))
    _store_heads(o_ref, outs)


def _nsa_prompt(q, kvb, gates, alpha_b, pe, w1, w2, w2t, biasc, bias_tiles, layer, batch, seq):
    nq = seq // ATT_TILE
    nb = seq // NSA_BLOCK
    nd = bias_tiles.shape[1]
    gw = NSA_GROUP * HEAD_DIM
    win0 = NSA_ROW_COLS // HEAD_DIM

    def col(c):
        return pl.BlockSpec((seq, HEAD_DIM), lambda b, g, i: (b, c + g))

    def per_layer(shape):
        return pl.BlockSpec((None,) + shape, lambda b, g, i: (layer,) + (0,) * len(shape))

    return pl.pallas_call(
        _nsa_prompt_kernel,
        grid=(batch, NSA_KV, nq),
        in_specs=[
            pl.BlockSpec((ATT_TILE, gw), lambda b, g, i: (b * nq + i, g)),
            col(0), col(NSA_KV), col(2 * NSA_KV), col(3 * NSA_KV),
            col(win0), col(win0 + NSA_KV),
            per_layer((2, NSA_BLOCK, HEAD_DIM)), per_layer((2, NSA_BLOCK, HEAD_DIM)),
            per_layer((2, HEAD_DIM, HEAD_DIM)), per_layer((2, HEAD_DIM, HEAD_DIM)),
            per_layer((2, HEAD_DIM, HEAD_DIM)),
            pl.BlockSpec((NSA_GROUP, nb, ATT_TILE), lambda b, g, i: (g, 0, i)),
            pl.BlockSpec((NSA_GROUP, nd, ATT_TILE, ATT_TILE), lambda b, g, i: (g, 0, 0, 0)),
            pl.BlockSpec((ATT_TILE, LANES), lambda b, g, i: (b * nq + i, g)),
        ],
        out_specs=pl.BlockSpec((ATT_TILE, gw), lambda b, g, i: (b * nq + i, g)),
        out_shape=jax.ShapeDtypeStruct((batch * seq, NSA_HEADS * HEAD_DIM), BF16),
        scratch_shapes=[
            pltpu.VMEM((nb, HEAD_DIM), BF16),
            pltpu.VMEM((HEAD_DIM, nb), BF16),
            pltpu.VMEM((HEAD_DIM, seq), BF16),
            pltpu.VMEM((HEAD_DIM, seq), BF16),
            pltpu.VMEM((NSA_GROUP, HEAD_DIM, ATT_TILE), F32),
            pltpu.VMEM((2 * NSA_GROUP, 1, ATT_TILE), F32),
            pltpu.VMEM((2 * NSA_GROUP, 1, ATT_TILE), F32),
            pltpu.VMEM((2 * NSA_GROUP, HEAD_DIM, ATT_TILE), F32),
        ],
        compiler_params=_cparams(("arbitrary", "arbitrary", "arbitrary")),
        name="nsa_prompt",
    )(q, kvb, kvb, kvb, kvb, kvb, kvb, alpha_b, pe, w1, w2, w2t, biasc, bias_tiles, gates)


def _compress_mlp(z, w1, w2):
    return _dot(_compress_hidden(z, w1), w2.astype(BF16))


SB_PAGES_PER_STEP = 16
MOBA_PAGES_PER_STEP = 16
NSA_PAGES_PER_STEP = 8


def _sb_sample_kernel(pt_ref, q_ref, *refs):
    del pt_ref
    pages = refs[:SB_PAGES_PER_STEP]
    o_ref, carry_s, acc_s = refs[SB_PAGES_PER_STEP:]
    s = pl.program_id(1)
    hw = SB_HEADS * HEAD_DIM
    page = pages[0].shape[0] // SB_SLABS

    @pl.when(s == 0)
    def _():
        carry_s[...] = jnp.zeros(carry_s.shape, F32)
        acc_s[...] = jnp.zeros(acc_s.shape, F32)

    own = _iota((SUBLANES, hw), 1) // HEAD_DIM == _iota((SUBLANES, hw), 0)
    q_bd = jnp.where(own, q_ref[...], 0.0).astype(BF16)
    u_incl = (_iota((page, page), 0) >= _iota((page, page), 1)).astype(BF16)

    def heads_of(ref, first):
        return jnp.concatenate([_slab(ref, first + h, page, SB_SLABS) for h in range(SB_HEADS)], axis=1).astype(BF16)

    z = [_dot_nt(q_bd, heads_of(ref, 0)) for ref in pages]
    sp = [_softplus(zz) for zz in z]
    incl = [_dot_split(-s_, u_incl) for s_ in sp]
    carry = carry_s[...]
    weights = []
    for zz, s_, inc in zip(z, sp, incl):
        weights.append(jnp.exp(zz - s_ + (carry + inc + s_)).astype(BF16))
        carry = carry + inc[:, 0:1]
    carry_s[...] = carry
    acc_s[...] += sum(_dot(a, heads_of(ref, SB_HEADS)) for a, ref in zip(weights, pages))

    @pl.when(s == pl.num_programs(1) - 1)
    def _():
        o_ref[...] = jnp.sum(jnp.where(own, acc_s[...], 0.0), axis=0, keepdims=True)


def _sb_sample(page_table, q3, cache_rows, layer):
    batch, n_pages = page_table.shape
    hw = SB_HEADS * HEAD_DIM
    steps = n_pages // SB_PAGES_PER_STEP

    def page_map(k):
        return lambda b, s, pt: (layer, pt[b, n_pages - 1 - (s * SB_PAGES_PER_STEP + k)], 0, 0)

    grid_spec = pltpu.PrefetchScalarGridSpec(
        num_scalar_prefetch=1,
        grid=(batch, steps),
        in_specs=[pl.BlockSpec((None, 1, hw), lambda b, s, pt: (b, 0, NSA_HEADS * HEAD_DIM // hw))]
        + [pl.BlockSpec((None, None) + cache_rows.shape[2:], page_map(k)) for k in range(SB_PAGES_PER_STEP)],
        out_specs=pl.BlockSpec((None, 1, hw), lambda b, s, pt: (b, 0, 0)),
        scratch_shapes=[pltpu.VMEM((SUBLANES, 1), F32), pltpu.VMEM((SUBLANES, hw), F32)],
    )
    return pl.pallas_call(
        _sb_sample_kernel,
        grid_spec=grid_spec,
        out_shape=jax.ShapeDtypeStruct((batch, 1, hw), F32),
        compiler_params=_cparams(("arbitrary", "arbitrary")),
        name="sb_sample",
    )(page_table, q3, *([cache_rows] * SB_PAGES_PER_STEP))


def _moba_select_kernel(pt_ref, q_ref, *refs):
    del pt_ref
    pages = refs[:MOBA_PAGES_PER_STEP]
    idx_ref, kmean_s = refs[MOBA_PAGES_PER_STEP:]
    s = pl.program_id(1)
    page = pages[0].shape[0]
    per_block = MOBA_BLOCK // page
    blocks = MOBA_PAGES_PER_STEP // per_block
    rows = blocks * MOBA_SLABS
    nb = kmean_s.shape[0] // MOBA_SLABS

    sums = [jnp.sum(ref[...], axis=0) for ref in pages]
    pad = jnp.zeros((MOBA_SLABS - MOBA_HEADS, HEAD_DIM), F32)
    means = [jnp.concatenate([sum(sums[r * per_block:(r + 1) * per_block]) * (1.0 / MOBA_BLOCK), pad], axis=0)
             for r in range(blocks)]
    kmean_s[pl.ds(pl.multiple_of(s * rows, rows), rows), :] = jnp.concatenate(means, axis=0)

    @pl.when(s == pl.num_programs(1) - 1)
    def _():
        prod = (kmean_s[...].reshape(nb, MOBA_SLABS, HEAD_DIM) * q_ref[...][None]).reshape(nb * MOBA_SLABS, HEAD_DIM)
        ones = jnp.ones((HEAD_DIM, LANES), BF16)
        gs = _dot_split(prod, ones, 3).reshape(nb, MOBA_SLABS, LANES)
        n_idx = _iota((nb, MOBA_SLABS, LANES), 0)
        rank = jnp.zeros((nb, MOBA_SLABS, LANES), F32)
        for mm in range(nb):
            gm = gs[mm][None]
            first = (mm < n_idx).astype(F32)
            rank += jnp.where(gm > gs, 1.0, jnp.where(gm == gs, first, 0.0))
        n_f = n_idx.astype(F32)
        for k in range(MOBA_TOPK):
            idx_ref[k] = jnp.sum(jnp.where(rank == float(k), n_f, 0.0), axis=0).astype(I32)


def _moba_select(page_table, q8, cache, layer):
    batch, n_pages = page_table.shape
    page = cache.shape[2]
    steps = n_pages // MOBA_PAGES_PER_STEP
    nb = n_pages * page // MOBA_BLOCK

    def page_map(k):
        return lambda b, s, pt: (layer, pt[b, s * MOBA_PAGES_PER_STEP + k], 0, 0, 0, 0)

    grid_spec = pltpu.PrefetchScalarGridSpec(
        num_scalar_prefetch=1,
        grid=(batch, steps),
        in_specs=[pl.BlockSpec((None, MOBA_SLABS, HEAD_DIM), lambda b, s, pt: (b, 0, 0))]
        + [pl.BlockSpec((None, None, page, None, MOBA_HEADS, HEAD_DIM), page_map(k))
           for k in range(MOBA_PAGES_PER_STEP)],
        out_specs=pl.BlockSpec((None, MOBA_TOPK, MOBA_SLABS, LANES), lambda b, s, pt: (b, 0, 0, 0)),
        scratch_shapes=[pltpu.VMEM((nb * MOBA_SLABS, HEAD_DIM), F32)],
    )
    return pl.pallas_call(
        _moba_select_kernel,
        grid_spec=grid_spec,
        out_shape=jax.ShapeDtypeStruct((batch, MOBA_TOPK, MOBA_SLABS, LANES), I32),
        compiler_params=_cparams(("arbitrary", "arbitrary")),
        name="moba_select",
    )(page_table, q8, *([cache] * MOBA_PAGES_PER_STEP))


def _moba_sample_kernel(pt_ref, idx_ref, q_ref, knew_ref, vnew_ref, bias0_ref, *refs):
    del pt_ref, idx_ref
    n = (len(refs) - 1) // 2
    page_refs, b_refs = refs[0:n], refs[n:2 * n]
    o_ref = refs[2 * n]
    h = pl.program_id(1)
    q = q_ref[...]
    q8 = jnp.broadcast_to(q, (SUBLANES, HEAD_DIM)).astype(BF16)
    page = page_refs[0].shape[0] // MOBA_SLABS
    s_new = jnp.sum(q * knew_ref[...], axis=-1, keepdims=True) + bias0_ref[:, 0:1]
    s = [(_dot_nt(q8, _slab(ref, h, page, MOBA_SLABS).astype(BF16)) + b_ref[...])[0:1, :]
         for ref, b_ref in zip(page_refs, b_refs)]
    m = s_new
    for s_ in s:
        m = jnp.maximum(m, jnp.max(s_, axis=-1, keepdims=True))
    p_new = jnp.exp(s_new - m)
    p = [jnp.exp(s_ - m) for s_ in s]
    den = p_new + sum(jnp.sum(p_, axis=-1, keepdims=True) for p_ in p)
    acc = p_new * vnew_ref[...]
    for p_, ref in zip(p, page_refs):
        p8 = jnp.broadcast_to(p_, (SUBLANES, page)).astype(BF16)
        acc = acc + _dot(p8, _slab(ref, MOBA_HEADS + h, page, MOBA_SLABS).astype(BF16))[0:1, :]
    o_ref[...] = acc / den


def _moba_sample(page_table, idx, q3, cache_rows, rows4, bias_pages, bias0, layer):
    batch, n_pages = page_table.shape
    page = cache_rows.shape[2] // MOBA_SLABS
    per_block = MOBA_BLOCK // page
    n = MOBA_TOPK * per_block

    def pg(b, h, u, ix):
        return ix[b, u // per_block, h] * per_block + u % per_block

    def page_spec(u):
        return pl.BlockSpec((None, None) + cache_rows.shape[2:],
                            lambda b, h, pt, ix: (layer, pt[b, pg(b, h, u, ix)], 0, 0))

    def bias_spec(u):
        return pl.BlockSpec((None, None, 1, page), lambda b, h, pt, ix: (h, pg(b, h, u, ix), 0, 0))

    grid_spec = pltpu.PrefetchScalarGridSpec(
        num_scalar_prefetch=2,
        grid=(batch, MOBA_HEADS),
        in_specs=[
            pl.BlockSpec((None, 1, HEAD_DIM), lambda b, h, pt, ix: (b, 0, NSA_HEADS + SB_HEADS + h)),
            pl.BlockSpec((None, None, 1, HEAD_DIM), lambda b, h, pt, ix: (b, h, 0, 0)),
            pl.BlockSpec((None, None, 1, HEAD_DIM), lambda b, h, pt, ix: (b, MOBA_HEADS + h, 0, 0)),
            pl.BlockSpec((None, 1, LANES), lambda b, h, pt, ix: (h, 0, 0)),
        ]
        + [page_spec(u) for u in range(n)] + [bias_spec(u) for u in range(n)],
        out_specs=pl.BlockSpec((None, 1, HEAD_DIM), lambda b, h, pt, ix: (b, 0, h)),
    )
    return pl.pallas_call(
        _moba_sample_kernel,
        grid_spec=grid_spec,
        out_shape=jax.ShapeDtypeStruct((batch, 1, MOBA_HEADS * HEAD_DIM), F32),
        compiler_params=_cparams(("arbitrary", "arbitrary")),
        name="moba_sample",
    )(page_table, idx, q3, rows4, rows4, bias0, *([cache_rows] * n), *([bias_pages] * n))


def _nsa_select_kernel(pt_ref, qg_ref, alpha_ref, pe_ref, w1_ref, w2_ref, biasc_ref, *refs):
    del pt_ref
    pages = refs[:NSA_PAGES_PER_STEP]
    ocmp_ref, idx_ref, z_s = refs[NSA_PAGES_PER_STEP:]
    s = pl.program_id(1)
    page = pages[0].shape[0]
    per_page = page // NSA_BLOCK
    rows = NSA_PAGES_PER_STEP * per_page
    nb = z_s.shape[0]

    alpha = alpha_ref[...]
    zs = [jnp.sum(ref[...].reshape((per_page, NSA_BLOCK) + ref.shape[1:]) * alpha[None], axis=1) for ref in pages]
    z_s[pl.ds(pl.multiple_of(s * rows, rows), rows)] = jnp.concatenate(zs, axis=0)

    @pl.when(s == pl.num_programs(1) - 1)
    def _():
        n_lane = _iota((1, nb), 1)
        pe_sum = jnp.sum(alpha * pe_ref[...], axis=0)
        for g in range(NSA_KV):
            kc, vc = [_compress_mlp(z_s[:, c * NSA_KV + g, :] + pe_sum[c * NSA_KV + g:c * NSA_KV + g + 1, :],
                                    w1_ref[c], w2_ref[c]).astype(BF16) for c in range(2)]
            lc = _dot_nt(qg_ref[g].astype(BF16), kc) + biasc_ref[g]
            lc = lc - jnp.max(lc, axis=-1, keepdims=True)
            e = jnp.exp(lc)
            p = e / jnp.maximum(jnp.sum(e, axis=-1, keepdims=True), 1e-30)
            ocmp_ref[g] = _dot(p.astype(BF16), vc)
            imp = jnp.sum(p[0:NSA_GROUP], axis=0, keepdims=True)
            by_lane = jnp.broadcast_to(imp, (nb, nb))
            by_row = by_lane.T
            m_idx = _iota((nb, nb), 0)
            n_idx = _iota((nb, nb), 1)
            beats = jnp.where(by_row > by_lane, 1.0,
                              jnp.where((by_row == by_lane) & (m_idx < n_idx), 1.0, 0.0))
            beats = jnp.where(m_idx >= 1, beats, 0.0)
            rank = jnp.sum(beats, axis=0, keepdims=True)
            k_idx = _iota((NSA_N_SEL, nb), 0)
            hit = ((jnp.broadcast_to(rank, (NSA_N_SEL, nb)) == k_idx.astype(F32)) & (n_lane >= 1)
                   & (k_idx < NSA_N_SEL - 2))
            pick = jnp.sum(jnp.where(hit, n_lane.astype(F32), 0.0), axis=-1, keepdims=True)
            idx_ref[g] = jnp.broadcast_to(pick, (NSA_N_SEL, LANES)).astype(I32)


def _nsa_select(page_table, qg, cache, alpha_w, pe_w, w1, w2, biasc_s, layer):
    batch, n_pages = page_table.shape
    page = cache.shape[2]
    steps = n_pages // NSA_PAGES_PER_STEP
    nb = n_pages * page // NSA_BLOCK
    cmp_block = (2 * NSA_KV, HEAD_DIM)

    def page_map(k):
        return lambda b, s, pt: (layer, pt[b, s * NSA_PAGES_PER_STEP + k], 0, 0, 0, 0)

    grid_spec = pltpu.PrefetchScalarGridSpec(
        num_scalar_prefetch=1,
        grid=(batch, steps),
        in_specs=[
            pl.BlockSpec((None, NSA_KV, SUBLANES, HEAD_DIM), lambda b, s, pt: (b, 0, 0, 0)),
            pl.BlockSpec((None, NSA_BLOCK) + cmp_block, lambda b, s, pt: (layer, 0, 0, 0)),
            pl.BlockSpec((None, NSA_BLOCK) + cmp_block, lambda b, s, pt: (layer, 0, 0, 0)),
            pl.BlockSpec((None, 2, HEAD_DIM, HEAD_DIM), lambda b, s, pt: (layer, 0, 0, 0)),
            pl.BlockSpec((None, 2, HEAD_DIM, HEAD_DIM), lambda b, s, pt: (layer, 0, 0, 0)),
            pl.BlockSpec((NSA_KV, SUBLANES, nb), lambda b, s, pt: (0, 0, 0)),
        ] + [pl.BlockSpec((None, None, page, None) + cmp_block, page_map(k)) for k in range(NSA_PAGES_PER_STEP)],
        out_specs=[
            pl.BlockSpec((None, NSA_KV, SUBLANES, HEAD_DIM), lambda b, s, pt: (b, 0, 0, 0)),
            pl.BlockSpec((None, NSA_KV, NSA_N_SEL, LANES), lambda b, s, pt: (b, 0, 0, 0)),
        ],
        scratch_shapes=[pltpu.VMEM((nb,) + cmp_block, F32)],
    )
    return pl.pallas_call(
        _nsa_select_kernel,
        grid_spec=grid_spec,
        out_shape=[jax.ShapeDtypeStruct((batch, NSA_KV, SUBLANES, HEAD_DIM), F32),
                   jax.ShapeDtypeStruct((batch, NSA_KV, NSA_N_SEL, LANES), I32)],
        compiler_params=_cparams(("arbitrary", "arbitrary")),
        name="nsa_select",
    )(page_table, qg, alpha_w, pe_w, w1, w2, biasc_s, *([cache] * NSA_PAGES_PER_STEP))


def _nsa_sample_kernel(pt_ref, idx_ref, qg_ref, ocmp_ref, gate_ref, sknew_ref, svnew_ref, wknew_ref, wvnew_ref,
                       win_ref, biasw_ref, bias0_ref, biasb_ref, *refs):
    del pt_ref
    blk_refs, o_ref = refs[:-1], refs[-1]
    b = pl.program_id(0)
    g = pl.program_id(1)
    qf = qg_ref[...]
    q = qf.astype(BF16)
    bias0 = bias0_ref[:, 0:1]

    s_own = jnp.sum(qf * sknew_ref[...], axis=-1, keepdims=True) + bias0
    s = [_dot_nt(q, _slab(ref, 2 * NSA_KV + g, NSA_BLOCK, NSA_SLABS).astype(BF16)) + biasb_ref[idx_ref[b, g, u]]
         for u, ref in enumerate(blk_refs)]
    m = s_own
    for s_ in s:
        m = jnp.maximum(m, jnp.max(s_, axis=-1, keepdims=True))
    p_own = jnp.exp(s_own - m)
    p = [jnp.exp(s_ - m) for s_ in s]
    l = p_own + sum(jnp.sum(p_, axis=-1, keepdims=True) for p_ in p)
    acc = p_own * svnew_ref[...]
    for p_, ref in zip(p, blk_refs):
        acc = acc + _dot(p_.astype(BF16), _slab(ref, 3 * NSA_KV + g, NSA_BLOCK, NSA_SLABS).astype(BF16))
    o_slc = acc / l

    w = win_ref.shape[0] // WIN_SLABS
    wk = _slab(win_ref, g, w, WIN_SLABS).astype(BF16)
    wv = _slab(win_ref, NSA_KV + g, w, WIN_SLABS).astype(BF16)
    sw = _dot_nt(q, wk) + biasw_ref[...]
    valid = _iota((1, w), 1) >= 1
    sw = jnp.where(valid, sw, NEG)
    s_new = jnp.sum(qf * wknew_ref[...], axis=-1, keepdims=True) + bias0
    mw = jnp.maximum(jnp.max(sw, axis=-1, keepdims=True), s_new)
    pw = jnp.where(valid, jnp.exp(sw - mw), 0.0)
    p_new = jnp.exp(s_new - mw)
    den = jnp.sum(pw, axis=-1, keepdims=True) + p_new
    o_win = (_dot(pw.astype(BF16), wv) + p_new * wvnew_ref[...]) / den

    gates = jnp.broadcast_to(gate_ref[...], (SUBLANES, LANES))
    lane = _iota((SUBLANES, LANES), 1)
    row = _iota((SUBLANES, LANES), 0)

    def gate(c):
        return jnp.sum(jnp.where(lane == c * NSA_GROUP + row, gates, 0.0), axis=-1, keepdims=True)

    o = gate(0) * ocmp_ref[...] + gate(1) * o_slc + gate(2) * o_win
    for jh in range(NSA_GROUP):
        o_ref[:, jh * HEAD_DIM:(jh + 1) * HEAD_DIM] = o[jh:jh + 1, :]


def _nsa_sample(page_table, idx, qg, ocmp, gates3, nsa_rows4, win_rows4, cache_blk, state_rows, bias_blocks,
                bias_w, bias0, layer, per_page):
    batch, n_pages = page_table.shape
    n_blk = NSA_N_SEL - 1
    w = state_rows.shape[2] // WIN_SLABS
    gw = NSA_GROUP * HEAD_DIM
    nb = bias_blocks.shape[1]

    def cache_spec(u):
        def index(b, g, pt, ix):
            n = ix[b, g, u]
            return (layer, pt[b, n // per_page] * per_page + n % per_page, 0, 0)
        return pl.BlockSpec((None, None) + cache_blk.shape[2:], index)

    def new_spec(slab):
        return pl.BlockSpec((None, None, 1, HEAD_DIM), lambda b, g, pt, ix: (b, slab + g, 0, 0))

    grid_spec = pltpu.PrefetchScalarGridSpec(
        num_scalar_prefetch=2,
        grid=(batch, NSA_KV),
        in_specs=[
            pl.BlockSpec((None, None, SUBLANES, HEAD_DIM), lambda b, g, pt, ix: (b, g, 0, 0)),
            pl.BlockSpec((None, None, SUBLANES, HEAD_DIM), lambda b, g, pt, ix: (b, g, 0, 0)),
            pl.BlockSpec((None, 1, LANES), lambda b, g, pt, ix: (b, 0, g)),
            new_spec(2 * NSA_KV), new_spec(3 * NSA_KV),
            new_spec(0), new_spec(NSA_KV),
            pl.BlockSpec((None, None) + state_rows.shape[2:], lambda b, g, pt, ix: (layer, b, 0, 0)),
            pl.BlockSpec((None, SUBLANES, w), lambda b, g, pt, ix: (g, 0, 0)),
            pl.BlockSpec((None, SUBLANES, LANES), lambda b, g, pt, ix: (g, 0, 0)),
            pl.BlockSpec((None, nb, SUBLANES, NSA_BLOCK), lambda b, g, pt, ix: (g, 0, 0, 0)),
        ]
        + [cache_spec(u) for u in range(n_blk)],
        out_specs=pl.BlockSpec((None, 1, gw), lambda b, g, pt, ix: (b, 0, g)),
    )
    return pl.pallas_call(
        _nsa_sample_kernel,
        grid_spec=grid_spec,
        out_shape=jax.ShapeDtypeStruct((batch, 1, NSA_HEADS * HEAD_DIM), F32),
        compiler_params=_cparams(("arbitrary", "arbitrary")),
        name="nsa_sample",
    )(page_table, idx, qg, ocmp, gates3, nsa_rows4, nsa_rows4, win_rows4, win_rows4, state_rows,
      bias_w, bias0, bias_blocks, *([cache_blk] * n_blk))


def _rel_bucket(dist):
    exact = REL_BUCKETS // 2
    d = jnp.maximum(dist, 0)
    df = jnp.maximum(d, 1).astype(F32)
    far = exact + (jnp.log(df / exact) / math.log(REL_MAX_DIST / exact) * (REL_BUCKETS - exact)).astype(I32)
    return jnp.where(d < exact, d, jnp.minimum(far, REL_BUCKETS - 1))


def _pad_rows(a, axis, n):
    pad = [(0, 0)] * a.ndim
    pad[axis] = (0, n - a.shape[axis])
    return jnp.pad(a, pad)


def kernel(x_prompt, x_sample, cache_nsa, cache_sb, cache_moba, state_nsa_win, page_table, norm_mix, w_in,
           nsa_cmp_pe, nsa_cmp_alpha, nsa_cmp_w1, nsa_cmp_w2, rel_bias, w_out, norm_ffn, w_up, w_down,
           norm_final):
    batch, seq, d = x_prompt.shape
    dec_batch, dec_seq, _ = x_sample.shape
    depth = w_in.shape[0]
    n_pool, page = cache_nsa.shape[1], cache_nsa.shape[2]
    n_pages = page_table.shape[1]
    past = n_pages * page
    w_buf = state_nsa_win.shape[2]
    assert dec_seq == 1 and seq % ATT_TILE == 0 and ATT_TILE == MOBA_BLOCK
    assert page % NSA_BLOCK == 0 and MOBA_BLOCK % page == 0 and w_buf == NSA_WINDOW
    assert past // NSA_BLOCK >= NSA_N_SEL and past // MOBA_BLOCK >= MOBA_TOPK
    assert dec_batch <= SAMPLE_ROWS and (batch * seq) % ROW_TILE == 0
    assert NSA_SLABS == SUBLANES and SB_SLABS == SUBLANES and MOBA_SLABS == SUBLANES

    o_gate = MIX_WIDTH + KV_COLS
    w_main = w_in[:, :, :o_gate].astype(BF16)
    wg = w_in[:, :, o_gate:].reshape(depth, d, 3, NSA_KV, NSA_GROUP).transpose(0, 1, 3, 2, 4)
    wg = wg.reshape(depth, d, NSA_KV, 3 * NSA_GROUP)
    w_gate = _pad_rows(wg, 3, LANES).reshape(depth, d, NSA_KV * LANES).astype(BF16)
    w_out_b = w_out.astype(BF16)
    w_up_b = w_up.astype(BF16)
    w_down_b = w_down.astype(BF16)
    g_mix = norm_mix.reshape(depth, 1, d)
    g_ffn = norm_ffn.reshape(depth, 1, d)
    g_fin = norm_final.reshape(1, d)
    alpha_b = jnp.broadcast_to(nsa_cmp_alpha[..., None], nsa_cmp_alpha.shape + (HEAD_DIM,))
    cmp_w2t = nsa_cmp_w2.transpose(0, 1, 3, 2)

    def slab_params(p):
        return jnp.repeat(p, NSA_KV, axis=1).transpose(0, 2, 1, 3)

    alpha_w = slab_params(alpha_b)
    pe_w = slab_params(nsa_cmp_pe)

    max_d = max(seq, past)
    bucket_1d = _rel_bucket(jnp.arange(max_d + 1, dtype=I32))
    thr = jnp.sum(bucket_1d[None, :] < jnp.arange(REL_BUCKETS, dtype=I32)[:, None], axis=1).astype(I32)
    tbl_t = rel_bias.T
    bias_tiles = _bias_tiles(thr, tbl_t, seq // ATT_TILE)
    biasc_p = _bias_cmp(thr, tbl_t[:NSA_HEADS], seq // NSA_BLOCK, seq)

    nb_s = past // NSA_BLOCK

    def heads8(a):
        a = a.reshape((NSA_KV, NSA_GROUP) + a.shape[1:])
        return _pad_rows(a, 1, SUBLANES)

    bias_back = rel_bias[_rel_bucket(past - jnp.arange(past, dtype=I32))].T
    nsa_back = bias_back[:NSA_HEADS]
    moba_back = bias_back[NSA_HEADS:]
    biasc_s = heads8(nsa_back[:, NSA_BLOCK - 1::NSA_BLOCK])
    bias_blocks = heads8(nsa_back.reshape(NSA_HEADS, nb_s, NSA_BLOCK)).transpose(0, 2, 1, 3)
    bias_w = heads8(rel_bias[_rel_bucket(NSA_WINDOW - jnp.arange(w_buf, dtype=I32)), :NSA_HEADS].T)
    bias0 = rel_bias[_rel_bucket(jnp.zeros((), I32))]
    bias0_nsa = jnp.broadcast_to(heads8(bias0[:NSA_HEADS])[..., None], (NSA_KV, SUBLANES, LANES))
    bias_pages = moba_back.reshape(MOBA_HEADS, n_pages, 1, page)
    bias0_moba = jnp.broadcast_to(bias0[NSA_HEADS:][:, None, None], (MOBA_HEADS, 1, LANES))

    per_page = page // NSA_BLOCK
    cache_nsa_half = cache_nsa.reshape(depth, n_pool, page, 2, NSA_SLABS // 2, HEAD_DIM)
    cache_nsa_blk =cache_nsa.reshape(depth, n_pool * per_page, NSA_BLOCK * NSA_SLABS, HEAD_DIM)
    cache_sb_rows = cache_sb.reshape(depth, n_pool, page * SB_SLABS, HEAD_DIM)
    cache_moba_rows = cache_moba.reshape(depth, n_pool, page * MOBA_SLABS, HEAD_DIM)
    state_rows = state_nsa_win.reshape(depth, dec_batch, w_buf * WIN_SLABS, HEAD_DIM)

    xp = x_prompt.reshape(batch * seq, d)
    xs = _pad_rows(x_sample.reshape(dec_batch, d), 0, SAMPLE_ROWS)
    win_keep = min(NSA_WINDOW, seq)
    kv_p = ()
    kv_s = ()
    win_s_out = []

    for l in range(depth):
        last = l == depth - 1
        q, *kv_p, gates, kvb = _in_proj(xp, g_mix, w_main, w_gate, l, depth, IN_ROW_TILE, kv_p)
        mix_nsa = _nsa_prompt(q, kvb, gates, alpha_b, nsa_cmp_pe, nsa_cmp_w1, nsa_cmp_w2, cmp_w2t,
                              biasc_p, bias_tiles, l, batch, seq)
        mix_sb = _sb_prompt(q, kvb, batch, seq)
        mix_moba = _moba_prompt(q, kvb, bias_tiles, batch, seq)
        xp = _out_proj(xp, mix_nsa, mix_sb, mix_moba, w_out_b, l, ROW_TILE)
        xp = _ffn(xp, g_ffn, w_up_b, w_down_b, g_fin, l, ROW_TILE, last)

        q_s, *kv_s, gates_s, _ = _in_proj(xs, g_mix, w_main, w_gate, l, depth, SAMPLE_ROWS, kv_s)
        nsa_s, win_s, _, moba_s = [a.reshape(depth, -1, HEAD_DIM)[l] for a in kv_s]
        q3 = q_s.astype(F32).reshape(SAMPLE_ROWS, 1, MIX_WIDTH)
        qh = q3[:dec_batch, 0].reshape(dec_batch, N_HEADS, HEAD_DIM)
        qg = _pad_rows(qh[:, :NSA_HEADS].reshape(dec_batch, NSA_KV, NSA_GROUP, HEAD_DIM), 2, SUBLANES)
        q_moba8 = _pad_rows(qh[:, NSA_HEADS + SB_HEADS:], 1, MOBA_SLABS)
        nsa_s4 = nsa_s.reshape(SAMPLE_ROWS, NSA_SLABS, 1, HEAD_DIM)
        win_s4 = win_s.reshape(SAMPLE_ROWS, WIN_SLABS, 1, HEAD_DIM)
        moba_s4 = moba_s.reshape(SAMPLE_ROWS, MOBA_SLABS, 1, HEAD_DIM)
        gates_s3 = gates_s.reshape(SAMPLE_ROWS, 1, 2 * LANES)

        ocmp, sel = _nsa_select(page_table, qg, cache_nsa_half, alpha_w, pe_w, nsa_cmp_w1, nsa_cmp_w2, biasc_s, l)
        o_nsa = _nsa_sample(page_table, sel[:, :, :, 0], qg, ocmp, gates_s3, nsa_s4, win_s4, cache_nsa_blk,
                            state_rows, bias_blocks, bias_w, bias0_nsa, l, per_page)
        o_sb = _sb_sample(page_table, q3, cache_sb_rows, l)
        top = _moba_select(page_table, q_moba8, cache_moba, l)
        o_moba = _moba_sample(page_table, top[:, :, :MOBA_HEADS, 0], q3, cache_moba_rows, moba_s4,
                              bias_pages, bias0_moba, l)

        def rows16(o):
            return _pad_rows(o.reshape(dec_batch, -1), 0, SAMPLE_ROWS).astype(BF16)

        xs = _out_proj(xs, rows16(o_nsa), rows16(o_sb), rows16(o_moba), w_out_b, l, SAMPLE_ROWS)
        xs = _ffn(xs, g_ffn, w_up_b, w_down_b, g_fin, l, SAMPLE_ROWS, last)
        new_win = win_s[:dec_batch * WIN_SLABS].reshape(dec_batch, 1, 2, NSA_KV, HEAD_DIM)
        win_s_out.append(jnp.concatenate([state_nsa_win[l][:, 1:], new_win], axis=1))

    def rows_p(a, *dims):
        return a.reshape((depth, batch, seq) + dims + (HEAD_DIM,))

    def rows_s(a, *dims):
        return a.reshape((depth, SAMPLE_ROWS, 1) + dims + (HEAD_DIM,))[:, :dec_batch]

    nsa_p, win_p, sb_p, moba_p = kv_p
    nsa_s, win_s, sb_s, moba_s = kv_s
    y_prompt = xp.reshape(batch, seq, d)
    y_sample = xs[:dec_batch].reshape(dec_batch, 1, d)
    return (y_prompt, y_sample,
            rows_p(nsa_p, 4, NSA_KV), rows_s(nsa_s, 4, NSA_KV),
            rows_p(sb_p, 2, SB_HEADS), rows_s(sb_s, 2, SB_HEADS),
            rows_p(moba_p, 2, MOBA_HEADS), rows_s(moba_s, 2, MOBA_HEADS),
            rows_p(win_p, 2, NSA_KV)[:, :, seq - win_keep:], jnp.stack(win_s_out))
```

```python
import functools
import math

import jax
import jax.numpy as jnp
from jax import lax
from jax.experimental import pallas as pl
from jax.experimental.pallas import tpu as pltpu

F32 = jnp.float32
BF16 = jnp.bfloat16
I32 = jnp.int32

HEAD_DIM = 128
NSA_HEADS = 8
SB_HEADS = 4
MOBA_HEADS = 4
NSA_GROUP = 4
NSA_KV = 2
N_HEADS = NSA_HEADS + SB_HEADS + MOBA_HEADS
MIX_WIDTH = N_HEADS * HEAD_DIM
NSA_SLABS = 4 * NSA_KV
WIN_SLABS = 2 * NSA_KV
SB_SLABS = 2 * SB_HEADS
MOBA_SLABS = 2 * MOBA_HEADS
NSA_ROW_COLS = NSA_SLABS * HEAD_DIM
WIN_ROW_COLS = WIN_SLABS * HEAD_DIM
SB_ROW_COLS = SB_SLABS * HEAD_DIM
MOBA_ROW_COLS = MOBA_SLABS * HEAD_DIM
KV_COLS = NSA_ROW_COLS + WIN_ROW_COLS + SB_ROW_COLS + MOBA_ROW_COLS
NSA_BLOCK = 64
NSA_N_SEL = 16
NSA_WINDOW = 512
MOBA_BLOCK = 256
MOBA_TOPK = 3
REL_BUCKETS = 32
REL_MAX_DIST = 2048
RMS_EPS = 1e-6
ATTN_SCALE = HEAD_DIM ** -0.5
NEG = -1e30
FORCED = 1e4

V7X_VMEM_LIMIT_BYTES = 56 * 1024 * 1024
LANES = 128
SUBLANES = 8
COL_TILE = 512
ROW_TILE = 1024
IN_ROW_TILE = 1024
ATT_TILE = 256
SAMPLE_ROWS = 16
SLABS_PER_TILE = COL_TILE // HEAD_DIM


def _cparams(sem):
    return pltpu.CompilerParams(dimension_semantics=sem, vmem_limit_bytes=V7X_VMEM_LIMIT_BYTES)


def _dot(a, b):
    return jnp.dot(a, b, preferred_element_type=F32)


def _dot_nt(a, b):
    return lax.dot_general(a, b, (((1,), (1,)), ((), ())), preferred_element_type=F32)


def _split_bf16(a, terms):
    parts = []
    for _ in range(terms):
        p = a.astype(BF16)
        parts.append(p)
        a = a - p.astype(F32)
    return parts


def _dot_split(a, b_bf16, terms=2):
    return sum(_dot(p, b_bf16) for p in _split_bf16(a, terms))


def _iota(shape, dim):
    return lax.broadcasted_iota(I32, shape, dim)


def _rms(x, g):
    return (x * lax.rsqrt(jnp.mean(x * x, axis=-1, keepdims=True) + RMS_EPS)) * g


def _softplus(z):
    return jnp.maximum(z, 0.0) + jnp.log(1.0 + jnp.exp(-jnp.abs(z)))


def _head(a, h):
    return a[:, h * HEAD_DIM:(h + 1) * HEAD_DIM]


def _slab(ref, s, n, slabs):
    return ref[pl.ds(s, n, stride=slabs), :]


N_Q_TILES = MIX_WIDTH // COL_TILE
N_KV_TILES = KV_COLS // COL_TILE
KV_GROUPS = (("nsa", NSA_SLABS, 0), ("win", WIN_SLABS, 2), ("sb", SB_SLABS, 3), ("moba", MOBA_SLABS, 5))


def _in_proj_kernel(x_ref, g_ref, w_ref, wg_ref, *refs):
    q_ref, nsa_ref, win_ref, sb_ref, moba_ref, gate_ref, kvb_ref, h_ref = refs[-8:]
    j = pl.program_id(1)
    tm = x_ref.shape[0]

    @pl.when(j == 0)
    def _():
        h = _rms(x_ref[...], g_ref[...]).astype(BF16)
        h_ref[...] = h
        gate_ref[...] = jax.nn.sigmoid(_dot(h, wg_ref[...]))

    acc = _dot(h_ref[...], w_ref[...])

    @pl.when(j < N_Q_TILES)
    def _():
        q_ref[...] = (acc * ATTN_SCALE).astype(BF16)

    @pl.when(j >= N_Q_TILES)
    def _():
        kvb_ref[...] = acc.astype(BF16)

    for (_, slabs, start), ref in zip(KV_GROUPS, (nsa_ref, win_ref, sb_ref, moba_ref)):
        for tt in range(slabs // SLABS_PER_TILE):
            @pl.when(j == N_Q_TILES + start + tt)
            def _(ref=ref, slabs=slabs, tt=tt):
                for s in range(SLABS_PER_TILE):
                    ref[pl.ds(tt * SLABS_PER_TILE + s, tm, stride=slabs), :] = _head(acc, s)


def _in_proj(x, g, w_main, w_gate, layer, depth, tm, prev_kv):
    m, d = x.shape
    n_tiles = N_Q_TILES + N_KV_TILES
    row_tiles = m // tm

    def clamp_map(lo, count):
        return lambda i, j: (i, jnp.clip(j - lo, 0, count - 1))

    out_shape = [jax.ShapeDtypeStruct((m, MIX_WIDTH), BF16)]
    out_specs = [pl.BlockSpec((tm, COL_TILE), clamp_map(0, N_Q_TILES))]
    for _, slabs, _ in KV_GROUPS:
        out_shape.append(jax.ShapeDtypeStruct((depth * m * slabs, HEAD_DIM), F32))
        out_specs.append(pl.BlockSpec((tm * slabs, HEAD_DIM), lambda i, j: (layer * row_tiles + i, 0),
                                      pipeline_mode=pl.Buffered(1)))
    out_shape += [jax.ShapeDtypeStruct((m, 2 * LANES), F32), jax.ShapeDtypeStruct((m, KV_COLS), BF16)]
    out_specs += [pl.BlockSpec((tm, 2 * LANES), lambda i, j: (i, 0)),
                  pl.BlockSpec((tm, COL_TILE), clamp_map(N_Q_TILES, N_KV_TILES))]
    n_in = 4
    return pl.pallas_call(
        _in_proj_kernel,
        grid=(row_tiles, n_tiles),
        in_specs=[
            pl.BlockSpec((tm, d), lambda i, j: (i, 0)),
            pl.BlockSpec((None, 1, d), lambda i, j: (layer, 0, 0)),
            pl.BlockSpec((None, d, COL_TILE), lambda i, j: (layer, 0, j)),
            pl.BlockSpec((None, d, 2 * LANES), lambda i, j: (layer, 0, 0)),
        ] + [pl.BlockSpec(memory_space=pl.ANY)] * len(prev_kv),
        out_specs=out_specs,
        out_shape=out_shape,
        input_output_aliases={n_in + k: 1 + k for k in range(len(prev_kv))},
        scratch_shapes=[pltpu.VMEM((tm, d), BF16)],
        compiler_params=_cparams(("arbitrary", "arbitrary")),
        name="in_proj",
    )(x, g, w_main, w_gate, *prev_kv)


def _out_proj_kernel(x_ref, a_ref, b_ref, c_ref, w_ref, o_ref):
    n_a = a_ref.shape[1]
    n_b = b_ref.shape[1]
    acc = _dot(a_ref[...], w_ref[0:n_a, :])
    acc += _dot(b_ref[...], w_ref[n_a:n_a + n_b, :])
    acc += _dot(c_ref[...], w_ref[n_a + n_b:, :])
    o_ref[...] = x_ref[...] + acc


def _out_proj(x, mix_nsa, mix_sb, mix_moba, w_out, layer, tm):
    m, d = x.shape
    return pl.pallas_call(
        _out_proj_kernel,
        grid=(m // tm, d // COL_TILE),
        in_specs=[
            pl.BlockSpec((tm, COL_TILE), lambda i, j: (i, j)),
            pl.BlockSpec((tm, mix_nsa.shape[1]), lambda i, j: (i, 0)),
            pl.BlockSpec((tm, mix_sb.shape[1]), lambda i, j: (i, 0)),
            pl.BlockSpec((tm, mix_moba.shape[1]), lambda i, j: (i, 0)),
            pl.BlockSpec((None, MIX_WIDTH, COL_TILE), lambda i, j: (layer, 0, j)),
        ],
        out_specs=pl.BlockSpec((tm, COL_TILE), lambda i, j: (i, j)),
        out_shape=jax.ShapeDtypeStruct((m, d), F32),
        compiler_params=_cparams(("arbitrary", "arbitrary")),
        name="out_proj",
    )(x, mix_nsa, mix_sb, mix_moba, w_out)


def _ffn_kernel(x_ref, g_ref, wu_ref, wd_ref, gf_ref, o_ref, h_ref, *, final_norm):
    j = pl.program_id(1)

    @pl.when(j == 0)
    def _():
        x = x_ref[...]
        h_ref[...] = _rms(x, g_ref[...]).astype(BF16)
        o_ref[...] = x

    a = jnp.maximum(_dot(h_ref[...], wu_ref[...]), 0.0)
    o_ref[...] += _dot((a * a).astype(BF16), wd_ref[...])

    if final_norm:
        @pl.when(j == pl.num_programs(1) - 1)
        def _():
            o_ref[...] = _rms(o_ref[...], gf_ref[...])


def _ffn(x, g, w_up, w_down, g_final, layer, tm, final_norm):
    m, d = x.shape
    d_ff = w_up.shape[2]
    return pl.pallas_call(
        functools.partial(_ffn_kernel, final_norm=final_norm),
        grid=(m // tm, d_ff // COL_TILE),
        in_specs=[
            pl.BlockSpec((tm, d), lambda i, j: (i, 0)),
            pl.BlockSpec((None, 1, d), lambda i, j: (layer, 0, 0)),
            pl.BlockSpec((None, d, COL_TILE), lambda i, j: (layer, 0, j)),
            pl.BlockSpec((None, COL_TILE, d), lambda i, j: (layer, j, 0)),
            pl.BlockSpec((1, d), lambda i, j: (0, 0)),
        ],
        out_specs=pl.BlockSpec((tm, d), lambda i, j: (i, 0)),
        out_shape=jax.ShapeDtypeStruct((m, d), F32),
        scratch_shapes=[pltpu.VMEM((tm, d), BF16)],
        compiler_params=_cparams(("arbitrary", "arbitrary")),
        name="ffn",
    )(x, g, w_up, w_down, g_final)


def _bias_grid_kernel(thr_ref, tbl_ref, o_ref, *, offset, step, row_step):
    h = pl.program_id(0)
    shape = o_ref.shape
    dist = offset + step * pl.program_id(1) + row_step * _iota(shape, 0) + _iota(shape, 1)
    acc = jnp.full(shape, tbl_ref[h, 0], F32)
    for k in range(1, REL_BUCKETS):
        acc = jnp.where(dist >= thr_ref[k], tbl_ref[h, k], acc)
    o_ref[...] = acc


def _bias_tiles(thr, tbl_t, nd):
    n_heads = tbl_t.shape[0]
    grid_spec = pltpu.PrefetchScalarGridSpec(
        num_scalar_prefetch=2,
        grid=(n_heads, nd),
        in_specs=[],
        out_specs=pl.BlockSpec((None, None, ATT_TILE, ATT_TILE), lambda h, d, thr, tbl: (h, d, 0, 0)),
    )
    return pl.pallas_call(
        functools.partial(_bias_grid_kernel, offset=0, step=ATT_TILE, row_step=-1),
        grid_spec=grid_spec,
        out_shape=jax.ShapeDtypeStruct((n_heads, nd, ATT_TILE, ATT_TILE), F32),
        compiler_params=_cparams(("arbitrary", "arbitrary")),
        name="bias_tiles",
    )(thr, tbl_t)


def _bias_cmp(thr, tbl_t, nb, seq):
    n_heads = tbl_t.shape[0]
    grid_spec = pltpu.PrefetchScalarGridSpec(
        num_scalar_prefetch=2,
        grid=(n_heads, seq // ATT_TILE),
        in_specs=[],
        out_specs=pl.BlockSpec((None, nb, ATT_TILE), lambda h, i, thr, tbl: (h, 0, i)),
    )
    return pl.pallas_call(
        functools.partial(_bias_grid_kernel, offset=-(NSA_BLOCK - 1), step=ATT_TILE, row_step=-NSA_BLOCK),
        grid_spec=grid_spec,
        out_shape=jax.ShapeDtypeStruct((n_heads, nb, seq), F32),
        compiler_params=_cparams(("arbitrary", "arbitrary")),
        name="bias_cmp",
    )(thr, tbl_t)


def _online_steps(m_s, l_s, acc_s, items):
    weights = []
    for slot, s, mask, _ in items:
        if mask is not None:
            s = jnp.where(mask, s, NEG)
        m_prev = m_s[slot]
        m_new = jnp.maximum(m_prev, jnp.max(s, axis=0, keepdims=True))
        alpha = jnp.exp(m_prev - m_new)
        p = jnp.exp(s - m_new)
        l_s[slot] = alpha * l_s[slot] + jnp.sum(p, axis=0, keepdims=True)
        m_s[slot] = m_new
        weights.append((alpha, p.astype(BF16)))
    for (slot, _, _, vt), (alpha, p) in zip(items, weights):
        acc_s[slot] = alpha * acc_s[slot] + _dot(vt, p)


def _reset_state(m_s, l_s, acc_s):
    m_s[...] = jnp.full(m_s.shape, NEG, F32)
    l_s[...] = jnp.zeros(l_s.shape, F32)
    acc_s[...] = jnp.zeros(acc_s.shape, F32)


def _tile(ref, j, t):
    return ref[pl.ds(pl.multiple_of(j * t, t), t), :]


def _tile_t(ref, j, t):
    return ref[:, pl.ds(pl.multiple_of(j * t, t), t)]


def _transpose_into(dst_ref, src_ref, heads):
    seq = src_ref.shape[0]
    for j in range(seq // ATT_TILE):
        rows = slice(j * ATT_TILE, (j + 1) * ATT_TILE)
        for h in range(heads):
            cols = slice(h * HEAD_DIM, (h + 1) * HEAD_DIM)
            dst_ref[cols, rows] = src_ref[rows, cols].astype(F32).T.astype(BF16)


def _rank_rows(score, n):
    idx = _iota((n, 1), 0)
    rank = jnp.zeros(score.shape, F32)
    for mm in range(n):
        sm = score[mm:mm + 1, :]
        first = (mm < idx).astype(F32)
        rank += jnp.where(sm > score, 1.0, jnp.where(sm == score, first, 0.0))
    return rank


def _store_heads(o_ref, outs_t):
    for h, o_t in enumerate(outs_t):
        o_ref[:, h * HEAD_DIM:(h + 1) * HEAD_DIM] = o_t.T.astype(o_ref.dtype)


def _sb_prompt_kernel(q_ref, k_ref, v_ref, o_ref, vt_s, carry_s, acc_s):
    t = ATT_TILE
    i = pl.program_id(1)

    @pl.when(i == 0)
    def _():
        _transpose_into(vt_s, v_ref, SB_HEADS)

    key = _iota((t, t), 0)
    qry = _iota((t, t), 1)
    u_suffix = (qry >= key).astype(BF16)
    strict = key < qry
    carry_s[...] = jnp.zeros(carry_s.shape, F32)
    acc_s[...] = jnp.zeros(acc_s.shape, F32)

    def step(j, mask):
        kt = _tile(k_ref, j, t)
        vt = _tile_t(vt_s, j, t)
        heads = range(SB_HEADS)
        z = [_dot_nt(_head(kt, h), _head(q_ref, h)) for h in heads]
        sp = [_softplus(z[h]) for h in heads]
        lk = [-sp[h] if mask is None else jnp.where(mask, -sp[h], 0.0) for h in heads]
        parts = [_split_bf16(lk[h], 2) for h in heads]
        incl = [sum(_dot(u_suffix, part) for part in parts[h]) for h in heads]
        weights = []
        for h in heads:
            carry = carry_s[h]
            a = jnp.exp(z[h] - sp[h] + (carry + incl[h] - lk[h]))
            if mask is not None:
                a = jnp.where(mask, a, 0.0)
            weights.append(a.astype(BF16))
            carry_s[h] = carry + incl[h][0:1, :]
        for h in heads:
            acc_s[h] += _dot(vt[h * HEAD_DIM:(h + 1) * HEAD_DIM, :], weights[h])

    step(i, strict)

    def body(d, _):
        step(i - d, None)
        return 0

    lax.fori_loop(1, i + 1, body, 0)
    _store_heads(o_ref, [acc_s[h] for h in range(SB_HEADS)])


def _sb_prompt(q, kvb, batch, seq):
    nq = seq // ATT_TILE
    hw = SB_HEADS * HEAD_DIM
    kv0 = (NSA_ROW_COLS + WIN_ROW_COLS) // hw
    q0 = NSA_HEADS * HEAD_DIM // hw
    return pl.pallas_call(
        _sb_prompt_kernel,
        grid=(batch, nq),
        in_specs=[
            pl.BlockSpec((ATT_TILE, hw), lambda b, i: (b * nq + i, q0)),
            pl.BlockSpec((seq, hw), lambda b, i: (b, kv0)),
            pl.BlockSpec((seq, hw), lambda b, i: (b, kv0 + 1)),
        ],
        out_specs=pl.BlockSpec((ATT_TILE, hw), lambda b, i: (b * nq + i, 0)),
        out_shape=jax.ShapeDtypeStruct((batch * seq, hw), BF16),
        scratch_shapes=[pltpu.VMEM((hw, seq), BF16),
                        pltpu.VMEM((SB_HEADS, 1, ATT_TILE), F32),
                        pltpu.VMEM((SB_HEADS, HEAD_DIM, ATT_TILE), F32)],
        compiler_params=_cparams(("arbitrary", "arbitrary")),
        name="sb_prompt",
    )(q, kvb, kvb)


def _moba_prompt_kernel(q_ref, k_ref, v_ref, bias_ref, o_ref, vt_s, kmean_s, sel_s, m_s, l_s, acc_s):
    t = ATT_TILE
    i = pl.program_id(1)
    seq = k_ref.shape[0]
    nb = seq // t

    @pl.when(i == 0)
    def _():
        _transpose_into(vt_s, v_ref, MOBA_HEADS)
        avg = jnp.where(_iota((nb, seq), 1) // t == _iota((nb, seq), 0), 1.0 / t, 0.0).astype(BF16)
        kmean_s[...] = _dot(avg, k_ref[...])

    nrow = _iota((nb, 1), 0)
    past = nrow < i
    for h in range(MOBA_HEADS):
        q = _head(q_ref, h)
        gs = sum(_dot_nt(part, q) for part in _split_bf16(_head(kmean_s, h), 2))
        rank = _rank_rows(jnp.where(past, gs, NEG), nb)
        sel_s[h] = jnp.where(past & (rank < MOBA_TOPK), 1.0, 0.0)

    _reset_state(m_s, l_s, acc_s)
    causal = _iota((t, t), 0) <= _iota((t, t), 1)

    def step(d, diagonal):
        j = i - d
        kt = _tile(k_ref, j, t)
        vt = _tile_t(vt_s, j, t)
        items = []
        for h in range(MOBA_HEADS):
            mask = causal if diagonal else sel_s[h, pl.ds(j, 1), :] > 0.5
            s = _dot_nt(_head(kt, h), _head(q_ref, h)) + bias_ref[h, d]
            items.append((h, s, mask, vt[h * HEAD_DIM:(h + 1) * HEAD_DIM, :]))
        _online_steps(m_s, l_s, acc_s, items)

    step(0, True)

    def body(d, _):
        step(d, False)
        return 0

    lax.fori_loop(1, i + 1, body, 0)
    _store_heads(o_ref, [acc_s[h] / l_s[h] for h in range(MOBA_HEADS)])


def _moba_prompt(q, kvb, bias_tiles, batch, seq):
    nq = seq // ATT_TILE
    hw = MOBA_HEADS * HEAD_DIM
    kv0 = (NSA_ROW_COLS + WIN_ROW_COLS + SB_ROW_COLS) // hw
    q0 = (NSA_HEADS + SB_HEADS) * HEAD_DIM // hw
    nd = bias_tiles.shape[1]
    nb = seq // MOBA_BLOCK
    return pl.pallas_call(
        _moba_prompt_kernel,
        grid=(batch, nq),
        in_specs=[
            pl.BlockSpec((ATT_TILE, hw), lambda b, i: (b * nq + i, q0)),
            pl.BlockSpec((seq, hw), lambda b, i: (b, kv0)),
            pl.BlockSpec((seq, hw), lambda b, i: (b, kv0 + 1)),
            pl.BlockSpec((MOBA_HEADS, nd, ATT_TILE, ATT_TILE), lambda b, i: (NSA_HEADS // MOBA_HEADS, 0, 0, 0)),
        ],
        out_specs=pl.BlockSpec((ATT_TILE, hw), lambda b, i: (b * nq + i, 0)),
        out_shape=jax.ShapeDtypeStruct((batch * seq, hw), BF16),
        scratch_shapes=[
            pltpu.VMEM((hw, seq), BF16),
            pltpu.VMEM((nb, hw), F32),
            pltpu.VMEM((MOBA_HEADS, nb, ATT_TILE), F32),
            pltpu.VMEM((MOBA_HEADS, 1, ATT_TILE), F32),
            pltpu.VMEM((MOBA_HEADS, 1, ATT_TILE), F32),
            pltpu.VMEM((MOBA_HEADS, HEAD_DIM, ATT_TILE), F32),
        ],
        compiler_params=_cparams(("arbitrary", "arbitrary")),
        name="moba_prompt",
    )(q, kvb, kvb, bias_tiles)


def _compress_hidden(z, w1):
    hid = _dot(z.astype(BF16), w1.astype(BF16))
    return (hid * jax.nn.sigmoid(hid)).astype(BF16)


def _nsa_prompt_kernel(q_ref, ck_ref, cv_ref, sk_ref, sv_ref, wk_ref, wv_ref, alpha_ref, pe_ref,
                       w1_ref, w2_ref, w2t_ref, biasc_ref, bias_ref, gate_ref, o_ref,
                       kc_s, vct_s, svt_s, wvt_s, ocmp_s, m_s, l_s, acc_s):
    t = ATT_TILE
    i = pl.program_id(2)
    seq = ck_ref.shape[0]
    nb = seq // NSA_BLOCK
    per_tile = t // NSA_BLOCK

    @pl.when(i == 0)
    def _():
        _transpose_into(svt_s, sv_ref, 1)
        _transpose_into(wvt_s, wv_ref, 1)
        hid = []
        for c, src in enumerate((ck_ref, cv_ref)):
            x = src[...].astype(F32).reshape(nb, NSA_BLOCK, HEAD_DIM)
            z = jnp.sum((x + pe_ref[c][None]) * alpha_ref[c][None], axis=1)
            hid.append(_compress_hidden(z, w1_ref[c]))
        kc_s[...] = _dot(hid[0], w2_ref[0].astype(BF16)).astype(BF16)
        vct_s[...] = _dot_nt(w2t_ref[1].astype(BF16), hid[1]).astype(BF16)

    tpos = i * t + _iota((1, t), 1)
    nrow = _iota((nb, 1), 0)
    complete = tpos >= nrow * NSA_BLOCK + (NSA_BLOCK - 1)
    kc = kc_s[...]
    vct = vct_s[...]
    imp = jnp.zeros((nb, t), F32)
    for jh in range(NSA_GROUP):
        z = jnp.where(complete, _dot_nt(kc, _head(q_ref, jh)) + biasc_ref[jh], NEG)
        z = z - jnp.max(z, axis=0, keepdims=True)
        e = jnp.where(complete, jnp.exp(z), 0.0)
        p = e / jnp.maximum(jnp.sum(e, axis=0, keepdims=True), 1e-30)
        imp += p
        ocmp_s[jh] = _dot(vct, p.astype(BF16))

    forced = (nrow == tpos // NSA_BLOCK) | (nrow == 0)
    score = jnp.where(forced, FORCED, jnp.where(complete, imp, NEG))
    rank = _rank_rows(score, nb)
    sel = jnp.where((rank < NSA_N_SEL) & (score > 0.5 * NEG), 1.0, 0.0).astype(BF16)

    def key_mask(j):
        expand = (_iota((t, nb), 1) == j * per_tile + _iota((t, nb), 0) // NSA_BLOCK).astype(BF16)
        return _dot(expand, sel) > 0.5

    _reset_state(m_s, l_s, acc_s)
    key = _iota((t, t), 0)
    qry = _iota((t, t), 1)

    def tile_step(d, with_window):
        j = i - d
        sk = _tile(sk_ref, j, t)
        svt = _tile_t(svt_s, j, t)
        slc_mask = key_mask(j)
        if with_window:
            wk = _tile(wk_ref, j, t)
            wvt = _tile_t(wvt_s, j, t)
            dist = d * t + qry - key
            win_mask = (dist >= 0) & (dist < NSA_WINDOW)
            slc_mask = slc_mask & (dist >= 0)
        items = []
        for jh in range(NSA_GROUP):
            qh = _head(q_ref, jh)
            bias = bias_ref[jh, d]
            items.append((jh, _dot_nt(sk, qh) + bias, slc_mask, svt))
            if with_window:
                items.append((NSA_GROUP + jh, _dot_nt(wk, qh) + bias, win_mask, wvt))
        _online_steps(m_s, l_s, acc_s, items)

    tile_step(0, True)
    n_win = NSA_WINDOW // t

    def body_win(d, _):
        tile_step(d, True)
        return 0

    def body_far(d, _):
        tile_step(d, False)
        return 0

    lax.fori_loop(1, jnp.minimum(i, n_win) + 1, body_win, 0)
    lax.fori_loop(n_win + 1, i + 1, body_far, 0)

    gates_t = gate_ref[...].T
    outs = []
    for jh in range(NSA_GROUP):
        g_cmp = gates_t[jh:jh + 1, :]
        g_slc = gates_t[NSA_GROUP + jh:NSA_GROUP + jh + 1, :]
        g_win = gates_t[2 * NSA_GROUP + jh:2 * NSA_GROUP + jh + 1, :]
        outs.append(g_cmp * ocmp_s[jh] + g_slc * (acc_s[jh] / l_s[jh])
                    + g_win * (acc_s[NSA_GROUP + jh] / l_s[NSA_GROUP + jh]))
    _store_heads(o_ref, outs)


def _nsa_prompt(q, kvb, gates, alpha_b, pe, w1, w2, w2t, biasc, bias_tiles, layer, batch, seq):
    nq = seq // ATT_TILE
    nb = seq // NSA_BLOCK
    nd = bias_tiles.shape[1]
    gw = NSA_GROUP * HEAD_DIM
    win0 = NSA_ROW_COLS // HEAD_DIM

    def col(c):
        return pl.BlockSpec((seq, HEAD_DIM), lambda b, g, i: (b, c + g))

    def per_layer(shape):
        return pl.BlockSpec((None,) + shape, lambda b, g, i: (layer,) + (0,) * len(shape))

    return pl.pallas_call(
        _nsa_prompt_kernel,
        grid=(batch, NSA_KV, nq),
        in_specs=[
            pl.BlockSpec((ATT_TILE, gw), lambda b, g, i: (b * nq + i, g)),
            col(0), col(NSA_KV), col(2 * NSA_KV), col(3 * NSA_KV),
            col(win0), col(win0 + NSA_KV),
            per_layer((2, NSA_BLOCK, HEAD_DIM)), per_layer((2, NSA_BLOCK, HEAD_DIM)),
            per_layer((2, HEAD_DIM, HEAD_DIM)), per_layer((2, HEAD_DIM, HEAD_DIM)),
            per_layer((2, HEAD_DIM, HEAD_DIM)),
            pl.BlockSpec((NSA_GROUP, nb, ATT_TILE), lambda b, g, i: (g, 0, i)),
            pl.BlockSpec((NSA_GROUP, nd, ATT_TILE, ATT_TILE), lambda b, g, i: (g, 0, 0, 0)),
            pl.BlockSpec((ATT_TILE, LANES), lambda b, g, i: (b * nq + i, g)),
        ],
        out_specs=pl.BlockSpec((ATT_TILE, gw), lambda b, g, i: (b * nq + i, g)),
        out_shape=jax.ShapeDtypeStruct((batch * seq, NSA_HEADS * HEAD_DIM), BF16),
        scratch_shapes=[
            pltpu.VMEM((nb, HEAD_DIM), BF16),
            pltpu.VMEM((HEAD_DIM, nb), BF16),
            pltpu.VMEM((HEAD_DIM, seq), BF16),
            pltpu.VMEM((HEAD_DIM, seq), BF16),
            pltpu.VMEM((NSA_GROUP, HEAD_DIM, ATT_TILE), F32),
            pltpu.VMEM((2 * NSA_GROUP, 1, ATT_TILE), F32),
            pltpu.VMEM((2 * NSA_GROUP, 1, ATT_TILE), F32),
            pltpu.VMEM((2 * NSA_GROUP, HEAD_DIM, ATT_TILE), F32),
        ],
        compiler_params=_cparams(("arbitrary", "arbitrary", "arbitrary")),
        name="nsa_prompt",
    )(q, kvb, kvb, kvb, kvb, kvb, kvb, alpha_b, pe, w1, w2, w2t, biasc, bias_tiles, gates)


def _compress_mlp(z, w1, w2):
    return _dot(_compress_hidden(z, w1), w2.astype(BF16))


SB_PAGES_PER_STEP = 16
MOBA_PAGES_PER_STEP = 16
NSA_PAGES_PER_STEP = 8


def _sb_sample_kernel(pt_ref, q_ref, *refs):
    del pt_ref
    pages = refs[:SB_PAGES_PER_STEP]
    o_ref, carry_s, acc_s = refs[SB_PAGES_PER_STEP:]
    s = pl.program_id(1)
    hw = SB_HEADS * HEAD_DIM
    page = pages[0].shape[0] // SB_SLABS

    @pl.when(s == 0)
    def _():
        carry_s[...] = jnp.zeros(carry_s.shape, F32)
        acc_s[...] = jnp.zeros(acc_s.shape, F32)

    own = _iota((SUBLANES, hw), 1) // HEAD_DIM == _iota((SUBLANES, hw), 0)
    q_bd = jnp.where(own, q_ref[...], 0.0).astype(BF16)
    u_incl = (_iota((page, page), 0) >= _iota((page, page), 1)).astype(BF16)

    def heads_of(ref, first):
        return jnp.concatenate([_slab(ref, first + h, page, SB_SLABS) for h in range(SB_HEADS)], axis=1).astype(BF16)

    z = [_dot_nt(q_bd, heads_of(ref, 0)) for ref in pages]
    sp = [_softplus(zz) for zz in z]
    incl = [_dot_split(-s_, u_incl) for s_ in sp]
    carry = carry_s[...]
    weights = []
    for zz, s_, inc in zip(z, sp, incl):
        weights.append(jnp.exp(zz - s_ + (carry + inc + s_)).astype(BF16))
        carry = carry + inc[:, 0:1]
    carry_s[...] = carry
    acc_s[...] += sum(_dot(a, heads_of(ref, SB_HEADS)) for a, ref in zip(weights, pages))

    @pl.when(s == pl.num_programs(1) - 1)
    def _():
        o_ref[...] = jnp.sum(jnp.where(own, acc_s[...], 0.0), axis=0, keepdims=True)


def _sb_sample(page_table, q3, cache_rows, layer):
    batch, n_pages = page_table.shape
    hw = SB_HEADS * HEAD_DIM
    steps = n_pages // SB_PAGES_PER_STEP

    def page_map(k):
        return lambda b, s, pt: (layer, pt[b, n_pages - 1 - (s * SB_PAGES_PER_STEP + k)], 0, 0)

    grid_spec = pltpu.PrefetchScalarGridSpec(
        num_scalar_prefetch=1,
        grid=(batch, steps),
        in_specs=[pl.BlockSpec((None, 1, hw), lambda b, s, pt: (b, 0, NSA_HEADS * HEAD_DIM // hw))]
        + [pl.BlockSpec((None, None) + cache_rows.shape[2:], page_map(k)) for k in range(SB_PAGES_PER_STEP)],
        out_specs=pl.BlockSpec((None, 1, hw), lambda b, s, pt: (b, 0, 0)),
        scratch_shapes=[pltpu.VMEM((SUBLANES, 1), F32), pltpu.VMEM((SUBLANES, hw), F32)],
    )
    return pl.pallas_call(
        _sb_sample_kernel,
        grid_spec=grid_spec,
        out_shape=jax.ShapeDtypeStruct((batch, 1, hw), F32),
        compiler_params=_cparams(("arbitrary", "arbitrary")),
        name="sb_sample",
    )(page_table, q3, *([cache_rows] * SB_PAGES_PER_STEP))


def _moba_select_kernel(pt_ref, q_ref, *refs):
    del pt_ref
    pages = refs[:MOBA_PAGES_PER_STEP]
    idx_ref, kmean_s = refs[MOBA_PAGES_PER_STEP:]
    s = pl.program_id(1)
    page = pages[0].shape[0]
    per_block = MOBA_BLOCK // page
    blocks = MOBA_PAGES_PER_STEP // per_block
    rows = blocks * MOBA_SLABS
    nb = kmean_s.shape[0] // MOBA_SLABS

    sums = [jnp.sum(ref[...], axis=0) for ref in pages]
    pad = jnp.zeros((MOBA_SLABS - MOBA_HEADS, HEAD_DIM), F32)
    means = [jnp.concatenate([sum(sums[r * per_block:(r + 1) * per_block]) * (1.0 / MOBA_BLOCK), pad], axis=0)
             for r in range(blocks)]
    kmean_s[pl.ds(pl.multiple_of(s * rows, rows), rows), :] = jnp.concatenate(means, axis=0)

    @pl.when(s == pl.num_programs(1) - 1)
    def _():
        prod = (kmean_s[...].reshape(nb, MOBA_SLABS, HEAD_DIM) * q_ref[...][None]).reshape(nb * MOBA_SLABS, HEAD_DIM)
        ones = jnp.ones((HEAD_DIM, LANES), BF16)
        gs = _dot_split(prod, ones, 3).reshape(nb, MOBA_SLABS, LANES)
        n_idx = _iota((nb, MOBA_SLABS, LANES), 0)
        rank = jnp.zeros((nb, MOBA_SLABS, LANES), F32)
        for mm in range(nb):
            gm = gs[mm][None]
            first = (mm < n_idx).astype(F32)
            rank += jnp.where(gm > gs, 1.0, jnp.where(gm == gs, first, 0.0))
        n_f = n_idx.astype(F32)
        for k in range(MOBA_TOPK):
            idx_ref[k] = jnp.sum(jnp.where(rank == float(k), n_f, 0.0), axis=0).astype(I32)


def _moba_select(page_table, q8, cache, layer):
    batch, n_pages = page_table.shape
    page = cache.shape[2]
    steps = n_pages // MOBA_PAGES_PER_STEP
    nb = n_pages * page // MOBA_BLOCK

    def page_map(k):
        return lambda b, s, pt: (layer, pt[b, s * MOBA_PAGES_PER_STEP + k], 0, 0, 0, 0)

    grid_spec = pltpu.PrefetchScalarGridSpec(
        num_scalar_prefetch=1,
        grid=(batch, steps),
        in_specs=[pl.BlockSpec((None, MOBA_SLABS, HEAD_DIM), lambda b, s, pt: (b, 0, 0))]
        + [pl.BlockSpec((None, None, page, None, MOBA_HEADS, HEAD_DIM), page_map(k))
           for k in range(MOBA_PAGES_PER_STEP)],
        out_specs=pl.BlockSpec((None, MOBA_TOPK, MOBA_SLABS, LANES), lambda b, s, pt: (b, 0, 0, 0)),
        scratch_shapes=[pltpu.VMEM((nb * MOBA_SLABS, HEAD_DIM), F32)],
    )
    return pl.pallas_call(
        _moba_select_kernel,
        grid_spec=grid_spec,
        out_shape=jax.ShapeDtypeStruct((batch, MOBA_TOPK, MOBA_SLABS, LANES), I32),
        compiler_params=_cparams(("arbitrary", "arbitrary")),
        name="moba_select",
    )(page_table, q8, *([cache] * MOBA_PAGES_PER_STEP))


def _moba_sample_kernel(pt_ref, idx_ref, q_ref, knew_ref, vnew_ref, bias0_ref, *refs):
    del pt_ref, idx_ref
    n = (len(refs) - 1) // 2
    page_refs, b_refs = refs[0:n], refs[n:2 * n]
    o_ref = refs[2 * n]
    h = pl.program_id(1)
    q = q_ref[...]
    q8 = jnp.broadcast_to(q, (SUBLANES, HEAD_DIM)).astype(BF16)
    page = page_refs[0].shape[0] // MOBA_SLABS
    s_new = jnp.sum(q * knew_ref[...], axis=-1, keepdims=True) + bias0_ref[:, 0:1]
    s = [(_dot_nt(q8, _slab(ref, h, page, MOBA_SLABS).astype(BF16)) + b_ref[...])[0:1, :]
         for ref, b_ref in zip(page_refs, b_refs)]
    m = s_new
    for s_ in s:
        m = jnp.maximum(m, jnp.max(s_, axis=-1, keepdims=True))
    p_new = jnp.exp(s_new - m)
    p = [jnp.exp(s_ - m) for s_ in s]
    den = p_new + sum(jnp.sum(p_, axis=-1, keepdims=True) for p_ in p)
    acc = p_new * vnew_ref[...]
    for p_, ref in zip(p, page_refs):
        p8 = jnp.broadcast_to(p_, (SUBLANES, page)).astype(BF16)
        acc = acc + _dot(p8, _slab(ref, MOBA_HEADS + h, page, MOBA_SLABS).astype(BF16))[0:1, :]
    o_ref[...] = acc / den


def _moba_sample(page_table, idx, q3, cache_rows, rows4, bias_pages, bias0, layer):
    batch, n_pages = page_table.shape
    page = cache_rows.shape[2] // MOBA_SLABS
    per_block = MOBA_BLOCK // page
    n = MOBA_TOPK * per_block

    def pg(b, h, u, ix):
        return ix[b, u // per_block, h] * per_block + u % per_block

    def page_spec(u):
        return pl.BlockSpec((None, None) + cache_rows.shape[2:],
                            lambda b, h, pt, ix: (layer, pt[b, pg(b, h, u, ix)], 0, 0))

    def bias_spec(u):
        return pl.BlockSpec((None, None, 1, page), lambda b, h, pt, ix: (h, pg(b, h, u, ix), 0, 0))

    grid_spec = pltpu.PrefetchScalarGridSpec(
        num_scalar_prefetch=2,
        grid=(batch, MOBA_HEADS),
        in_specs=[
            pl.BlockSpec((None, 1, HEAD_DIM), lambda b, h, pt, ix: (b, 0, NSA_HEADS + SB_HEADS + h)),
            pl.BlockSpec((None, None, 1, HEAD_DIM), lambda b, h, pt, ix: (b, h, 0, 0)),
            pl.BlockSpec((None, None, 1, HEAD_DIM), lambda b, h, pt, ix: (b, MOBA_HEADS + h, 0, 0)),
            pl.BlockSpec((None, 1, LANES), lambda b, h, pt, ix: (h, 0, 0)),
        ]
        + [page_spec(u) for u in range(n)] + [bias_spec(u) for u in range(n)],
        out_specs=pl.BlockSpec((None, 1, HEAD_DIM), lambda b, h, pt, ix: (b, 0, h)),
    )
    return pl.pallas_call(
        _moba_sample_kernel,
        grid_spec=grid_spec,
        out_shape=jax.ShapeDtypeStruct((batch, 1, MOBA_HEADS * HEAD_DIM), F32),
        compiler_params=_cparams(("arbitrary", "arbitrary")),
        name="moba_sample",
    )(page_table, idx, q3, rows4, rows4, bias0, *([cache_rows] * n), *([bias_pages] * n))


def _nsa_select_kernel(pt_ref, qg_ref, alpha_ref, pe_ref, w1_ref, w2_ref, biasc_ref, *refs):
    del pt_ref
    pages = refs[:NSA_PAGES_PER_STEP]
    ocmp_ref, idx_ref, z_s = refs[NSA_PAGES_PER_STEP:]
    s = pl.program_id(1)
    page = pages[0].shape[0]
    per_page = page // NSA_BLOCK
    rows = NSA_PAGES_PER_STEP * per_page
    nb = z_s.shape[0]

    alpha = alpha_ref[...]
    zs = [jnp.sum(ref[...].reshape((per_page, NSA_BLOCK) + ref.shape[1:]) * alpha[None], axis=1) for ref in pages]
    z_s[pl.ds(pl.multiple_of(s * rows, rows), rows)] = jnp.concatenate(zs, axis=0)

    @pl.when(s == pl.num_programs(1) - 1)
    def _():
        n_lane = _iota((1, nb), 1)
        pe_sum = jnp.sum(alpha * pe_ref[...], axis=0)
        for g in range(NSA_KV):
            kc, vc = [_compress_mlp(z_s[:, c * NSA_KV + g, :] + pe_sum[c * NSA_KV + g:c * NSA_KV + g + 1, :],
                                    w1_ref[c], w2_ref[c]).astype(BF16) for c in range(2)]
            lc = _dot_nt(qg_ref[g].astype(BF16), kc) + biasc_ref[g]
            lc = lc - jnp.max(lc, axis=-1, keepdims=True)
            e = jnp.exp(lc)
            p = e / jnp.maximum(jnp.sum(e, axis=-1, keepdims=True), 1e-30)
            ocmp_ref[g] = _dot(p.astype(BF16), vc)
            imp = jnp.sum(p[0:NSA_GROUP], axis=0, keepdims=True)
            by_lane = jnp.broadcast_to(imp, (nb, nb))
            by_row = by_lane.T
            m_idx = _iota((nb, nb), 0)
            n_idx = _iota((nb, nb), 1)
            beats = jnp.where(by_row > by_lane, 1.0,
                              jnp.where((by_row == by_lane) & (m_idx < n_idx), 1.0, 0.0))
            beats = jnp.where(m_idx >= 1, beats, 0.0)
            rank = jnp.sum(beats, axis=0, keepdims=True)
            k_idx = _iota((NSA_N_SEL, nb), 0)
            hit = ((jnp.broadcast_to(rank, (NSA_N_SEL, nb)) == k_idx.astype(F32)) & (n_lane >= 1)
                   & (k_idx < NSA_N_SEL - 2))
            pick = jnp.sum(jnp.where(hit, n_lane.astype(F32), 0.0), axis=-1, keepdims=True)
            idx_ref[g] = jnp.broadcast_to(pick, (NSA_N_SEL, LANES)).astype(I32)


def _nsa_select(page_table, qg, cache, alpha_w, pe_w, w1, w2, biasc_s, layer):
    batch, n_pages = page_table.shape
    page = cache.shape[2]
    steps = n_pages // NSA_PAGES_PER_STEP
    nb = n_pages * page // NSA_BLOCK
    cmp_block = (2 * NSA_KV, HEAD_DIM)

    def page_map(k):
        return lambda b, s, pt: (layer, pt[b, s * NSA_PAGES_PER_STEP + k], 0, 0, 0, 0)

    grid_spec = pltpu.PrefetchScalarGridSpec(
        num_scalar_prefetch=1,
        grid=(batch, steps),
        in_specs=[
            pl.BlockSpec((None, NSA_KV, SUBLANES, HEAD_DIM), lambda b, s, pt: (b, 0, 0, 0)),
            pl.BlockSpec((None, NSA_BLOCK) + cmp_block, lambda b, s, pt: (layer, 0, 0, 0)),
            pl.BlockSpec((None, NSA_BLOCK) + cmp_block, lambda b, s, pt: (layer, 0, 0, 0)),
            pl.BlockSpec((None, 2, HEAD_DIM, HEAD_DIM), lambda b, s, pt: (layer, 0, 0, 0)),
            pl.BlockSpec((None, 2, HEAD_DIM, HEAD_DIM), lambda b, s, pt: (layer, 0, 0, 0)),
            pl.BlockSpec((NSA_KV, SUBLANES, nb), lambda b, s, pt: (0, 0, 0)),
        ] + [pl.BlockSpec((None, None, page, None) + cmp_block, page_map(k)) for k in range(NSA_PAGES_PER_STEP)],
        out_specs=[
            pl.BlockSpec((None, NSA_KV, SUBLANES, HEAD_DIM), lambda b, s, pt: (b, 0, 0, 0)),
            pl.BlockSpec((None, NSA_KV, NSA_N_SEL, LANES), lambda b, s, pt: (b, 0, 0, 0)),
        ],
        scratch_shapes=[pltpu.VMEM((nb,) + cmp_block, F32)],
    )
    return pl.pallas_call(
        _nsa_select_kernel,
        grid_spec=grid_spec,
        out_shape=[jax.ShapeDtypeStruct((batch, NSA_KV, SUBLANES, HEAD_DIM), F32),
                   jax.ShapeDtypeStruct((batch, NSA_KV, NSA_N_SEL, LANES), I32)],
        compiler_params=_cparams(("arbitrary", "arbitrary")),
        name="nsa_select",
    )(page_table, qg, alpha_w, pe_w, w1, w2, biasc_s, *([cache] * NSA_PAGES_PER_STEP))


def _nsa_sample_kernel(pt_ref, idx_ref, qg_ref, ocmp_ref, gate_ref, sknew_ref, svnew_ref, wknew_ref, wvnew_ref,
                       win_ref, biasw_ref, bias0_ref, biasb_ref, *refs):
    del pt_ref
    blk_refs, o_ref = refs[:-1], refs[-1]
    b = pl.program_id(0)
    g = pl.program_id(1)
    qf = qg_ref[...]
    q = qf.astype(BF16)
    bias0 = bias0_ref[:, 0:1]

    s_own = jnp.sum(qf * sknew_ref[...], axis=-1, keepdims=True) + bias0
    s = [_dot_nt(q, _slab(ref, 2 * NSA_KV + g, NSA_BLOCK, NSA_SLABS).astype(BF16)) + biasb_ref[idx_ref[b, g, u]]
         for u, ref in enumerate(blk_refs)]
    m = s_own
    for s_ in s:
        m = jnp.maximum(m, jnp.max(s_, axis=-1, keepdims=True))
    p_own = jnp.exp(s_own - m)
    p = [jnp.exp(s_ - m) for s_ in s]
    l = p_own + sum(jnp.sum(p_, axis=-1, keepdims=True) for p_ in p)
    acc = p_own * svnew_ref[...]
    for p_, ref in zip(p, blk_refs):
        acc = acc + _dot(p_.astype(BF16), _slab(ref, 3 * NSA_KV + g, NSA_BLOCK, NSA_SLABS).astype(BF16))
    o_slc = acc / l

    w = win_ref.shape[0] // WIN_SLABS
    wk = _slab(win_ref, g, w, WIN_SLABS).astype(BF16)
    wv = _slab(win_ref, NSA_KV + g, w, WIN_SLABS).astype(BF16)
    sw = _dot_nt(q, wk) + biasw_ref[...]
    valid = _iota((1, w), 1) >= 1
    sw = jnp.where(valid, sw, NEG)
    s_new = jnp.sum(qf * wknew_ref[...], axis=-1, keepdims=True) + bias0
    mw = jnp.maximum(jnp.max(sw, axis=-1, keepdims=True), s_new)
    pw = jnp.where(valid, jnp.exp(sw - mw), 0.0)
    p_new = jnp.exp(s_new - mw)
    den = jnp.sum(pw, axis=-1, keepdims=True) + p_new
    o_win = (_dot(pw.astype(BF16), wv) + p_new * wvnew_ref[...]) / den

    gates = jnp.broadcast_to(gate_ref[...], (SUBLANES, LANES))
    lane = _iota((SUBLANES, LANES), 1)
    row = _iota((SUBLANES, LANES), 0)

    def gate(c):
        return jnp.sum(jnp.where(lane == c * NSA_GROUP + row, gates, 0.0), axis=-1, keepdims=True)

    o = gate(0) * ocmp_ref[...] + gate(1) * o_slc + gate(2) * o_win
    for jh in range(NSA_GROUP):
        o_ref[:, jh * HEAD_DIM:(jh + 1) * HEAD_DIM] = o[jh:jh + 1, :]


def _nsa_sample(page_table, idx, qg, ocmp, gates3, nsa_rows4, win_rows4, cache_blk, state_rows, bias_blocks,
                bias_w, bias0, layer, per_page):
    batch, n_pages = page_table.shape
    n_blk = NSA_N_SEL - 1
    w = state_rows.shape[2] // WIN_SLABS
    gw = NSA_GROUP * HEAD_DIM
    nb = bias_blocks.shape[1]

    def cache_spec(u):
        def index(b, g, pt, ix):
            n = ix[b, g, u]
            return (layer, pt[b, n // per_page] * per_page + n % per_page, 0, 0)
        return pl.BlockSpec((None, None) + cache_blk.shape[2:], index)

    def new_spec(slab):
        return pl.BlockSpec((None, None, 1, HEAD_DIM), lambda b, g, pt, ix: (b, slab + g, 0, 0))

    grid_spec = pltpu.PrefetchScalarGridSpec(
        num_scalar_prefetch=2,
        grid=(batch, NSA_KV),
        in_specs=[
            pl.BlockSpec((None, None, SUBLANES, HEAD_DIM), lambda b, g, pt, ix: (b, g, 0, 0)),
            pl.BlockSpec((None, None, SUBLANES, HEAD_DIM), lambda b, g, pt, ix: (b, g, 0, 0)),
            pl.BlockSpec((None, 1, LANES), lambda b, g, pt, ix: (b, 0, g)),
            new_spec(2 * NSA_KV), new_spec(3 * NSA_KV),
            new_spec(0), new_spec(NSA_KV),
            pl.BlockSpec((None, None) + state_rows.shape[2:], lambda b, g, pt, ix: (layer, b, 0, 0)),
            pl.BlockSpec((None, SUBLANES, w), lambda b, g, pt, ix: (g, 0, 0)),
            pl.BlockSpec((None, SUBLANES, LANES), lambda b, g, pt, ix: (g, 0, 0)),
            pl.BlockSpec((None, nb, SUBLANES, NSA_BLOCK), lambda b, g, pt, ix: (g, 0, 0, 0)),
        ]
        + [cache_spec(u) for u in range(n_blk)],
        out_specs=pl.BlockSpec((None, 1, gw), lambda b, g, pt, ix: (b, 0, g)),
    )
    return pl.pallas_call(
        _nsa_sample_kernel,
        grid_spec=grid_spec,
        out_shape=jax.ShapeDtypeStruct((batch, 1, NSA_HEADS * HEAD_DIM), F32),
        compiler_params=_cparams(("arbitrary", "arbitrary")),
        name="nsa_sample",
    )(page_table, idx, qg, ocmp, gates3, nsa_rows4, nsa_rows4, win_rows4, win_rows4, state_rows,
      bias_w, bias0, bias_blocks, *([cache_blk] * n_blk))


def _rel_bucket(dist):
    exact = REL_BUCKETS // 2
    d = jnp.maximum(dist, 0)
    df = jnp.maximum(d, 1).astype(F32)
    far = exact + (jnp.log(df / exact) / math.log(REL_MAX_DIST / exact) * (REL_BUCKETS - exact)).astype(I32)
    return jnp.where(d < exact, d, jnp.minimum(far, REL_BUCKETS - 1))


def _pad_rows(a, axis, n):
    pad = [(0, 0)] * a.ndim
    pad[axis] = (0, n - a.shape[axis])
    return jnp.pad(a, pad)


def kernel(x_prompt, x_sample, cache_nsa, cache_sb, cache_moba, state_nsa_win, page_table, norm_mix, w_in,
           nsa_cmp_pe, nsa_cmp_alpha, nsa_cmp_w1, nsa_cmp_w2, rel_bias, w_out, norm_ffn, w_up, w_down,
           norm_final):
    batch, seq, d = x_prompt.shape
    dec_batch, dec_seq, _ = x_sample.shape
    depth = w_in.shape[0]
    n_pool, page = cache_nsa.shape[1], cache_nsa.shape[2]
    n_pages = page_table.shape[1]
    past = n_pages * page
    w_buf = state_nsa_win.shape[2]
    assert dec_seq == 1 and seq % ATT_TILE == 0 and ATT_TILE == MOBA_BLOCK
    assert page % NSA_BLOCK == 0 and MOBA_BLOCK % page == 0 and w_buf == NSA_WINDOW
    assert past // NSA_BLOCK >= NSA_N_SEL and past // MOBA_BLOCK >= MOBA_TOPK
    assert dec_batch <= SAMPLE_ROWS and (batch * seq) % ROW_TILE == 0
    assert NSA_SLABS == SUBLANES and SB_SLABS == SUBLANES and MOBA_SLABS == SUBLANES

    o_gate = MIX_WIDTH + KV_COLS
    w_main = w_in[:, :, :o_gate].astype(BF16)
    wg = w_in[:, :, o_gate:].reshape(depth, d, 3, NSA_KV, NSA_GROUP).transpose(0, 1, 3, 2, 4)
    wg = wg.reshape(depth, d, NSA_KV, 3 * NSA_GROUP)
    w_gate = _pad_rows(wg, 3, LANES).reshape(depth, d, NSA_KV * LANES).astype(BF16)
    w_out_b = w_out.astype(BF16)
    w_up_b = w_up.astype(BF16)
    w_down_b = w_down.astype(BF16)
    g_mix = norm_mix.reshape(depth, 1, d)
    g_ffn = norm_ffn.reshape(depth, 1, d)
    g_fin = norm_final.reshape(1, d)
    alpha_b = jnp.broadcast_to(nsa_cmp_alpha[..., None], nsa_cmp_alpha.shape + (HEAD_DIM,))
    cmp_w2t = nsa_cmp_w2.transpose(0, 1, 3, 2)

    def slab_params(p):
        return jnp.repeat(p, NSA_KV, axis=1).transpose(0, 2, 1, 3)

    alpha_w = slab_params(alpha_b)
    pe_w = slab_params(nsa_cmp_pe)

    max_d = max(seq, past)
    bucket_1d = _rel_bucket(jnp.arange(max_d + 1, dtype=I32))
    thr = jnp.sum(bucket_1d[None, :] < jnp.arange(REL_BUCKETS, dtype=I32)[:, None], axis=1).astype(I32)
    tbl_t = rel_bias.T
    bias_tiles = _bias_tiles(thr, tbl_t, seq // ATT_TILE)
    biasc_p = _bias_cmp(thr, tbl_t[:NSA_HEADS], seq // NSA_BLOCK, seq)

    nb_s = past // NSA_BLOCK

    def heads8(a):
        a = a.reshape((NSA_KV, NSA_GROUP) + a.shape[1:])
        return _pad_rows(a, 1, SUBLANES)

    bias_back = rel_bias[_rel_bucket(past - jnp.arange(past, dtype=I32))].T
    nsa_back = bias_back[:NSA_HEADS]
    moba_back = bias_back[NSA_HEADS:]
    biasc_s = heads8(nsa_back[:, NSA_BLOCK - 1::NSA_BLOCK])
    bias_blocks = heads8(nsa_back.reshape(NSA_HEADS, nb_s, NSA_BLOCK)).transpose(0, 2, 1, 3)
    bias_w = heads8(rel_bias[_rel_bucket(NSA_WINDOW - jnp.arange(w_buf, dtype=I32))].T[:NSA_HEADS])
    bias0 = rel_bias[_rel_bucket(jnp.zeros((), I32))]
    bias0_nsa = jnp.broadcast_to(heads8(bias0[:NSA_HEADS])[..., None], (NSA_KV, SUBLANES, LANES))
    bias_pages = moba_back.reshape(MOBA_HEADS, n_pages, 1, page)
    bias0_moba = jnp.broadcast_to(bias0[NSA_HEADS:][:, None, None], (MOBA_HEADS, 1, LANES))

    per_page = page // NSA_BLOCK
    cache_nsa_half = cache_nsa.reshape(depth, n_pool, page, 2, NSA_SLABS // 2, HEAD_DIM)
    cache_nsa_blk =cache_nsa.reshape(depth, n_pool * per_page, NSA_BLOCK * NSA_SLABS, HEAD_DIM)
    cache_sb_rows = cache_sb.reshape(depth, n_pool, page * SB_SLABS, HEAD_DIM)
    cache_moba_rows = cache_moba.reshape(depth, n_pool, page * MOBA_SLABS, HEAD_DIM)
    state_rows = state_nsa_win.reshape(depth, dec_batch, w_buf * WIN_SLABS, HEAD_DIM)

    xp = x_prompt.reshape(batch * seq, d)
    xs = _pad_rows(x_sample.reshape(dec_batch, d), 0, SAMPLE_ROWS)
    win_keep = min(NSA_WINDOW, seq)
    kv_p = ()
    kv_s = ()
    win_s_out = []

    for l in range(depth):
        last = l == depth - 1
        q, *kv_p, gates, kvb = _in_proj(xp, g_mix, w_main, w_gate, l, depth, IN_ROW_TILE, kv_p)
        mix_nsa = _nsa_prompt(q, kvb, gates, alpha_b, nsa_cmp_pe, nsa_cmp_w1, nsa_cmp_w2, cmp_w2t,
                              biasc_p, bias_tiles, l, batch, seq)
        mix_sb = _sb_prompt(q, kvb, batch, seq)
        mix_moba = _moba_prompt(q, kvb, bias_tiles, batch, seq)
        xp = _out_proj(xp, mix_nsa, mix_sb, mix_moba, w_out_b, l, ROW_TILE)
        xp = _ffn(xp, g_ffn, w_up_b, w_down_b, g_fin, l, ROW_TILE, last)

        q_s, *kv_s, gates_s, _ = _in_proj(xs, g_mix, w_main, w_gate, l, depth, SAMPLE_ROWS, kv_s)
        nsa_s, win_s, _, moba_s = [a.reshape(depth, -1, HEAD_DIM)[l] for a in kv_s]
        q3 = q_s.astype(F32).reshape(SAMPLE_ROWS, 1, MIX_WIDTH)
        qh = q3[:dec_batch, 0].reshape(dec_batch, N_HEADS, HEAD_DIM)
        qg = _pad_rows(qh[:, :NSA_HEADS].reshape(dec_batch, NSA_KV, NSA_GROUP, HEAD_DIM), 2, SUBLANES)
        q_moba8 = _pad_rows(qh[:, NSA_HEADS + SB_HEADS:], 1, MOBA_SLABS)
        nsa_s4 = nsa_s.reshape(SAMPLE_ROWS, NSA_SLABS, 1, HEAD_DIM)
        win_s4 = win_s.reshape(SAMPLE_ROWS, WIN_SLABS, 1, HEAD_DIM)
        moba_s4 = moba_s.reshape(SAMPLE_ROWS, MOBA_SLABS, 1, HEAD_DIM)
        gates_s3 = gates_s.reshape(SAMPLE_ROWS, 1, 2 * LANES)

        ocmp, sel = _nsa_select(page_table, qg, cache_nsa_half, alpha_w, pe_w, nsa_cmp_w1, nsa_cmp_w2, biasc_s, l)
        o_nsa = _nsa_sample(page_table, sel[:, :, :, 0], qg, ocmp, gates_s3, nsa_s4, win_s4, cache_nsa_blk,
                            state_rows, bias_blocks, bias_w, bias0_nsa, l, per_page)
        o_sb = _sb_sample(page_table, q3, cache_sb_rows, l)
        top = _moba_select(page_table, q_moba8, cache_moba, l)
        o_moba = _moba_sample(page_table, top[:, :, :MOBA_HEADS, 0], q3, cache_moba_rows, moba_s4,
                              bias_pages, bias0_moba, l)

        def rows16(o):
            return _pad_rows(o.reshape(dec_batch, -1), 0, SAMPLE_ROWS).astype(BF16)

        xs = _out_proj(xs, rows16(o_nsa), rows16(o_sb), rows16(o_moba), w_out_b, l, SAMPLE_ROWS)
        xs = _ffn(xs, g_ffn, w_up_b, w_down_b, g_fin, l, SAMPLE_ROWS, last)
        new_win = win_s[:dec_batch * WIN_SLABS].reshape(dec_batch, 1, 2, NSA_KV, HEAD_DIM)
        win_s_out.append(jnp.concatenate([state_nsa_win[l][:, 1:], new_win], axis=1))

    def rows_p(a, *dims):
        return a.reshape((depth, batch, seq) + dims + (HEAD_DIM,))

    def rows_s(a, *dims):
        return a.reshape((depth, SAMPLE_ROWS, 1) + dims + (HEAD_DIM,))[:, :dec_batch]

    nsa_p, win_p, sb_p, moba_p = kv_p
    nsa_s, win_s, sb_s, moba_s = kv_s
    y_prompt = xp.reshape(batch, seq, d)
    y_sample = xs[:dec_batch].reshape(dec_batch, 1, d)
    return (y_prompt, y_sample,
            rows_p(nsa_p, 4, NSA_KV), rows_s(nsa_s, 4, NSA_KV),
            rows_p(sb_p, 2, SB_HEADS), rows_s(sb_s, 2, SB_HEADS),
            rows_p(moba_p, 2, MOBA_HEADS), rows_s(moba_s, 2, MOBA_HEADS),
            rows_p(win_p, 2, NSA_KV)[:, :, seq - win_keep:], jnp.stack(win_s_out))
```
